```python
import math
import jax
import jax.numpy as jnp
from jax import lax
import numpy as np

D_MODEL = 1024
BATCH = 4
SEQ = 8192
DEPTH = 4

N_MIXERS = 3

GDN_DK = 128
GDN_DV = 128
GDN_HEADS = D_MODEL // GDN_DK
GDN_CONV = 4
GDN_CHUNK = 64
GDN_QK = GDN_HEADS * GDN_DK
GDN_V = GDN_HEADS * GDN_DV
GDN_IN = 2 * GDN_QK + 2 * GDN_V + 2 * GDN_HEADS

MOBA_DH = 128
MOBA_HEADS = D_MODEL // MOBA_DH
MOBA_BLOCK = 256
MOBA_TOPK = 3
MOBA_QBLOCK = 16
ROPE_THETA = 500000.0
ROPE_DIMS = MOBA_DH // 4

RET_DK = 256
RET_DV = 512
RET_HEADS = D_MODEL // RET_DK
RET_CHUNK = 128
XPOS_BASE = 10000.0

D_FF = ((8 * D_MODEL + 3 * 256 - 1) // (3 * 256)) * 256

DEEPNORM_ALPHA = (2 * DEPTH) ** 0.25
DEEPNORM_BETA = (8 * DEPTH) ** -0.25
LN_EPS = 1e-5
NORM_EPS = 1e-6
NEG_INF = -1e30
F32 = jnp.float32

kernel_name = 'hybrid_gdn_moba_retention_deepnorm'


def layer_norm(x, g, b):
    xf = x.astype(F32)
    mu = jnp.mean(xf, -1, keepdims=True)
    var = jnp.mean(jnp.square(xf - mu), -1, keepdims=True)
    return ((xf - mu) * lax.rsqrt(var + LN_EPS) * g.astype(F32) + b.astype(F32)).astype(x.dtype)


def split_heads(t, n_heads):
    bsz, seq, width = t.shape
    return t.reshape(bsz, seq, n_heads, width // n_heads).transpose(0, 2, 1, 3)


def l2_normalize(t):
    tf = t.astype(F32)
    return tf * lax.rsqrt(jnp.sum(tf * tf, -1, keepdims=True) + NORM_EPS)


def apply_rotary(x, inv_freq):
    half = inv_freq.shape[0]
    seq = x.shape[2]
    ang = jnp.arange(seq, dtype=F32)[:, None] * inv_freq[None, :]
    cos, sin = jnp.cos(ang), jnp.sin(ang)
    xf = x.astype(F32)
    x1, x2 = xf[..., :half], xf[..., half:2 * half]
    out = jnp.concatenate([x1 * cos - x2 * sin, x1 * sin + x2 * cos, xf[..., 2 * half:]], -1)
    return out.astype(x.dtype)


def causal_depthwise_conv(x, w):
    width, ch = w.shape
    return lax.conv_general_dilated(
        x, w[:, None, :], window_strides=(1,), padding=[(width - 1, 0)],
        dimension_numbers=('NWC', 'WIO', 'NWC'), feature_group_count=ch)


def chunk_gated_delta_rule(q, k, v, g, beta):
    bsz, nh, seq, dk = q.shape
    dv = v.shape[-1]
    c = GDN_CHUNK
    n = seq // c
    q = q.astype(F32).reshape(bsz, nh, n, c, dk)
    k = k.astype(F32).reshape(bsz, nh, n, c, dk)
    v = v.astype(F32).reshape(bsz, nh, n, c, dv)
    gc = jnp.cumsum(g.reshape(bsz, nh, n, c), -1)
    beta = beta.reshape(bsz, nh, n, c)
    tril = jnp.tril(jnp.ones((c, c), bool))
    tril_strict = jnp.tril(jnp.ones((c, c), bool), -1)
    diff = gc[..., :, None] - gc[..., None, :]
    decay = jnp.where(tril, jnp.exp(jnp.where(tril, diff, 0.0)), 0.0)
    k_beta = k * beta[..., None]
    v_beta = v * beta[..., None]
    low = jnp.where(tril_strict, jnp.einsum('bhncd,bhnsd->bhncs', k_beta, k) * decay, 0.0)
    rhs = jnp.concatenate([v_beta, k_beta * jnp.exp(gc)[..., None]], -1)
    sol = lax.linalg.triangular_solve(low, rhs, left_side=True, lower=True, unit_diagonal=True)
    u, w = sol[..., :dv], sol[..., dv:]
    attn_intra = jnp.where(tril, jnp.einsum('bhncd,bhnsd->bhncs', q, k) * decay, 0.0)
    q_dec = q * jnp.exp(gc)[..., None]
    k_dec = k * jnp.exp(gc[..., -1:] - gc)[..., None]
    chunk_decay = jnp.exp(gc[..., -1])

    def step(state, inp):
        u_i, w_i, qd_i, kd_i, a_i, cd_i = inp
        v_new = u_i - w_i @ state
        o_i = qd_i @ state + a_i @ v_new
        state = state * cd_i[..., None, None] + jnp.swapaxes(kd_i, -1, -2) @ v_new
        return state, o_i

    xs = tuple(jnp.moveaxis(t, 2, 0) for t in (u, w, q_dec, k_dec, attn_intra, chunk_decay))
    state0 = jnp.zeros((bsz, nh, dk, dv), F32)
    _, o = lax.scan(step, state0, xs)
    return o.transpose(1, 2, 0, 3, 4).reshape(bsz, nh, seq, dv)


def gated_deltanet(x, w_in, w_conv, a_log, dt_bias, norm_g, w_out):
    bsz, seq, _ = x.shape
    nh = GDN_HEADS
    proj = x @ w_in
    qkv, z, b_logit, a_in = jnp.split(
        proj, [2 * GDN_QK + GDN_V, 2 * GDN_QK + 2 * GDN_V, 2 * GDN_QK + 2 * GDN_V + nh], axis=-1)
    qkv = jax.nn.silu(causal_depthwise_conv(qkv, w_conv))
    q, k, v = jnp.split(qkv, [GDN_QK, 2 * GDN_QK], axis=-1)
    q = l2_normalize(split_heads(q, nh)) * GDN_DK ** -0.5
    k = l2_normalize(split_heads(k, nh))
    v = split_heads(v, nh).astype(F32)
    beta = jax.nn.sigmoid(b_logit.astype(F32)).transpose(0, 2, 1)
    g = -(jnp.exp(a_log.astype(F32)) *
          jax.nn.softplus(a_in.astype(F32) + dt_bias.astype(F32))).transpose(0, 2, 1)
    o = chunk_gated_delta_rule(q, k, v, g, beta).transpose(0, 2, 1, 3)
    o = o * lax.rsqrt(jnp.mean(o * o, -1, keepdims=True) + NORM_EPS) * norm_g.astype(F32)
    o = o * jax.nn.silu(z.astype(F32)).reshape(bsz, seq, nh, GDN_DV)
    return o.reshape(bsz, seq, GDN_V).astype(x.dtype) @ w_out


def moba_attention(x, w_qkv, w_out):
    bsz, seq, _ = x.shape
    nh, dh, bs, qb = MOBA_HEADS, MOBA_DH, MOBA_BLOCK, MOBA_QBLOCK
    q, k, v = jnp.split(x @ w_qkv, 3, axis=-1)
    half = ROPE_DIMS // 2
    inv_freq = jnp.power(ROPE_THETA, -jnp.arange(half, dtype=F32) / half)
    q = apply_rotary(split_heads(q, nh), inv_freq)
    k = apply_rotary(split_heads(k, nh), inv_freq)
    v = split_heads(v, nh)
    nkb = -(-seq // bs)
    pad = nkb * bs - seq
    kp = jnp.pad(k, ((0, 0), (0, 0), (0, pad), (0, 0)))
    vp = jnp.pad(v, ((0, 0), (0, 0), (0, pad), (0, 0)))
    k_blocks = kp.reshape(bsz, nh, nkb, bs, dh)
    v_blocks = vp.reshape(bsz, nh, nkb, bs, dh)
    k_mean = jnp.mean(k_blocks.astype(F32), axis=3)
    topk = min(MOBA_TOPK, nkb)
    scale = dh ** -0.5
    b_idx = jnp.arange(bsz)[:, None, None, None]
    h_idx = jnp.arange(nh)[None, :, None, None]

    def one_query_block(qi):
        q0 = qi * qb
        own = q0 // bs
        q_blk = lax.dynamic_slice_in_dim(q, q0, qb, axis=2).astype(F32)
        gate = jnp.einsum('bhqd,bhnd->bhqn', q_blk, k_mean)
        gate = jnp.where(jnp.arange(nkb) < own, gate, -jnp.inf)
        _, sel = lax.top_k(gate, topk)
        sel_valid = sel < own
        k_sel = k_blocks[b_idx, h_idx, sel].astype(F32)
        v_sel = v_blocks[b_idx, h_idx, sel].astype(F32)
        s_sel = jnp.einsum('bhqd,bhqnkd->bhqnk', q_blk, k_sel) * scale
        s_sel = jnp.where(sel_valid[..., None], s_sel, NEG_INF)
        k_own = lax.dynamic_slice_in_dim(kp, own * bs, bs, axis=2).astype(F32)
        v_own = lax.dynamic_slice_in_dim(vp, own * bs, bs, axis=2).astype(F32)
        s_own = jnp.einsum('bhqd,bhkd->bhqk', q_blk, k_own) * scale
        q_pos = q0 + jnp.arange(qb)
        k_pos = own * bs + jnp.arange(bs)
        s_own = jnp.where(k_pos[None, :] <= q_pos[:, None], s_own, NEG_INF)
        s = jnp.concatenate([s_sel.reshape(bsz, nh, qb, topk * bs), s_own], -1)
        p = jax.nn.softmax(s, axis=-1)
        p_sel = p[..., :topk * bs].reshape(bsz, nh, qb, topk, bs)
        p_own = p[..., topk * bs:]
        return (jnp.einsum('bhqnk,bhqnkd->bhqd', p_sel, v_sel) +
                jnp.einsum('bhqk,bhkd->bhqd', p_own, v_own))

    o = lax.map(one_query_block, jnp.arange(seq // qb))
    o = o.transpose(1, 0, 3, 2, 4).reshape(bsz, seq, nh * dh)
    return o.astype(x.dtype) @ w_out


def chunk_retention(q, k, v, log_gamma):
    bsz, nh, seq, dk = q.shape
    dv = v.shape[-1]
    c = RET_CHUNK
    n = seq // c
    q = q.astype(F32).reshape(bsz, nh, n, c, dk)
    k = k.astype(F32).reshape(bsz, nh, n, c, dk)
    v = v.astype(F32).reshape(bsz, nh, n, c, dv)
    pos = jnp.arange(c, dtype=F32)
    tril = jnp.tril(jnp.ones((c, c), bool))
    rel = jnp.where(tril, pos[:, None] - pos[None, :], 0.0)
    dmat = jnp.where(tril, jnp.exp(log_gamma[:, None, None] * rel), 0.0)
    inner = jnp.einsum('bhncd,bhnsd->bhncs', q, k) * dmat[None, :, None]
    o_inner = jnp.einsum('bhncs,bhnsd->bhncd', inner, v)
    q_dec = q * jnp.exp(log_gamma[:, None] * (pos + 1.0))[None, :, None, :, None]
    k_dec = k * jnp.exp(log_gamma[:, None] * (c - 1.0 - pos))[None, :, None, :, None]
    chunk_decay = jnp.exp(log_gamma * c)[None, :, None, None]

    def step(state, inp):
        qd_i, kd_i, v_i = inp
        out = qd_i @ state
        state = state * chunk_decay + jnp.swapaxes(kd_i, -1, -2) @ v_i
        return state, out

    xs = tuple(jnp.moveaxis(t, 2, 0) for t in (q_dec, k_dec, v))
    state0 = jnp.zeros((bsz, nh, dk, dv), F32)
    _, o_cross = lax.scan(step, state0, xs)
    o = o_inner + o_cross.transpose(1, 2, 0, 3, 4)
    return o.reshape(bsz, nh, seq, dv)


def retention(x, w_in, gn_g, w_out):
    bsz, seq, _ = x.shape
    nh = RET_HEADS
    q, k, v, gate = jnp.split(x @ w_in, [nh * RET_DK, 2 * nh * RET_DK, 2 * nh * RET_DK + nh * RET_DV], axis=-1)
    inv_freq = jnp.power(XPOS_BASE, -jnp.linspace(0.0, 1.0, RET_DK // 2, dtype=F32))
    q = apply_rotary(split_heads(q, nh), inv_freq)
    k = apply_rotary(split_heads(k, nh), inv_freq).astype(F32) * RET_DK ** -0.5
    v = split_heads(v, nh)
    log_gamma = jnp.log1p(-jnp.exp2(-5.0 - jnp.arange(nh, dtype=F32)))
    o = chunk_retention(q, k, v, log_gamma).transpose(0, 2, 1, 3)
    mu = jnp.mean(o, -1, keepdims=True)
    var = jnp.mean(jnp.square(o - mu), -1, keepdims=True)
    o = ((o - mu) * lax.rsqrt(var + LN_EPS)).reshape(bsz, seq, nh * RET_DV) * gn_g.astype(F32)
    o = jax.nn.silu(gate.astype(F32)) * o
    return o.astype(x.dtype) @ w_out


def swiglu(x, w13, w2):
    g, u = jnp.split(x @ w13, 2, axis=-1)
    return (jax.nn.silu(g) * u) @ w2


def setup_inputs(seed: int = 0) -> dict:
    key = jax.random.key(seed)
    ks = jax.random.split(key, 24)
    n_a = len(range(0, DEPTH, N_MIXERS))
    n_b = len(range(1, DEPTH, N_MIXERS))
    n_c = len(range(2, DEPTH, N_MIXERS))

    def dense(k, shape, fan_in, gain=1.0):
        return jax.random.normal(k, shape, F32) * (gain * fan_in ** -0.5)

    def gain(k, shape):
        return 1.0 + 0.02 * jax.random.normal(k, shape, F32)

    x = jax.random.normal(ks[0], (BATCH, SEQ, D_MODEL), F32)
    a_w_in = dense(ks[1], (n_a, D_MODEL, GDN_IN), D_MODEL)
    a_conv = dense(ks[2], (n_a, GDN_CONV, 2 * GDN_QK + GDN_V), GDN_CONV)
    a_a_log = jnp.log(jax.random.uniform(ks[3], (n_a, GDN_HEADS), F32, 1.0, 16.0))
    dt = jnp.exp(jax.random.uniform(ks[4], (n_a, GDN_HEADS), F32, math.log(1e-3), math.log(1e-1)))
    a_dt_bias = dt + jnp.log(-jnp.expm1(-dt))
    a_norm_g = gain(ks[5], (n_a, GDN_DV))
    a_w_out = dense(ks[6], (n_a, GDN_V, D_MODEL), GDN_V, DEEPNORM_BETA)
    b_w_qkv = dense(ks[7], (n_b, D_MODEL, 3 * MOBA_HEADS * MOBA_DH), D_MODEL)
    b_w_out = dense(ks[8], (n_b, MOBA_HEADS * MOBA_DH, D_MODEL), MOBA_HEADS * MOBA_DH, DEEPNORM_BETA)
    c_w_in = dense(ks[9], (n_c, D_MODEL, 2 * RET_HEADS * RET_DK + 2 * RET_HEADS * RET_DV), D_MODEL)
    c_gn_g = gain(ks[10], (n_c, RET_HEADS * RET_DV))
    c_w_out = dense(ks[11], (n_c, RET_HEADS * RET_DV, D_MODEL), RET_HEADS * RET_DV, DEEPNORM_BETA)
    f_w13 = dense(ks[12], (DEPTH, D_MODEL, 2 * D_FF), D_MODEL)
    f_w2 = dense(ks[13], (DEPTH, D_FF, D_MODEL), D_FF, DEEPNORM_BETA)
    ln1_g = gain(ks[14], (DEPTH, D_MODEL))
    ln1_b = 0.02 * jax.random.normal(ks[15], (DEPTH, D_MODEL), F32)
    ln2_g = gain(ks[16], (DEPTH, D_MODEL))
    ln2_b = 0.02 * jax.random.normal(ks[17], (DEPTH, D_MODEL), F32)
    return {'x': x, 'a_w_in': a_w_in, 'a_conv': a_conv, 'a_a_log': a_a_log, 'a_dt_bias': a_dt_bias,
            'a_norm_g': a_norm_g, 'a_w_out': a_w_out, 'b_w_qkv': b_w_qkv, 'b_w_out': b_w_out,
            'c_w_in': c_w_in, 'c_gn_g': c_gn_g, 'c_w_out': c_w_out, 'f_w13': f_w13, 'f_w2': f_w2,
            'ln1_g': ln1_g, 'ln1_b': ln1_b, 'ln2_g': ln2_g, 'ln2_b': ln2_b}


def reference(x, a_w_in, a_conv, a_a_log, a_dt_bias, a_norm_g, a_w_out, b_w_qkv, b_w_out,
              c_w_in, c_gn_g, c_w_out, f_w13, f_w2, ln1_g, ln1_b, ln2_g, ln2_b):
    h = x
    for i in range(DEPTH):
        kind, j = i % N_MIXERS, i // N_MIXERS
        if kind == 0:
            mix = gated_deltanet(h, a_w_in[j], a_conv[j], a_a_log[j], a_dt_bias[j], a_norm_g[j], a_w_out[j])
        elif kind == 1:
            mix = moba_attention(h, b_w_qkv[j], b_w_out[j])
        else:
            mix = retention(h, c_w_in[j], c_gn_g[j], c_w_out[j])
        h = layer_norm(DEEPNORM_ALPHA * h + mix, ln1_g[i], ln1_b[i])
        h = layer_norm(DEEPNORM_ALPHA * h + swiglu(h, f_w13[i], f_w2[i]), ln2_g[i], ln2_b[i])
    return h
```

```python
import functools
import math

import jax
import jax.numpy as jnp
from jax import lax
from jax.experimental import pallas as pl
from jax.experimental.pallas import tpu as pltpu

F32 = jnp.float32
BF16 = jnp.bfloat16

N_MIXERS = 3
LANES = 128
VMEM_LIMIT = 56 * 1024 * 1024

GDN_DK = 128
GDN_CHUNK = 64
GDN_SUPER = 128
GDN_CONV = 4
MOBA_DH = 128
MOBA_BLOCK = 256
MOBA_TOPK = 3
ROPE_THETA = 500000.0
ROPE_DIMS = MOBA_DH // 4
RET_DK = 256
RET_DV = 512
RET_TILE = 256
XPOS_BASE = 10000.0
LN_EPS = 1e-5
NORM_EPS = 1e-6
NEG_INF = -1e30

NT_DIMS = (((1,), (1,)), ((), ()))
TN_DIMS = (((0,), (0,)), ((), ()))


def _params(*semantics):
    return pltpu.CompilerParams(dimension_semantics=semantics, vmem_limit_bytes=VMEM_LIMIT)


def _silu(x):
    return x * jax.nn.sigmoid(x)


def _proj_kernel(x_ref, w_ref, o_ref):
    x = x_ref[...].astype(BF16)
    o_ref[...] = jnp.dot(x, w_ref[...], preferred_element_type=F32).astype(o_ref.dtype)


def _proj(x2d, w, out_dtype, tm, tn):
    m, k = x2d.shape
    n = w.shape[1]
    return pl.pallas_call(
        _proj_kernel,
        grid=(m // tm, n // tn),
        in_specs=[pl.BlockSpec((tm, k), lambda i, j: (i, 0)),
                  pl.BlockSpec((k, tn), lambda i, j: (0, j))],
        out_specs=pl.BlockSpec((tm, tn), lambda i, j: (i, j)),
        out_shape=jax.ShapeDtypeStruct((m, n), out_dtype),
        compiler_params=_params("parallel", "parallel"),
        name="proj",
    )(x2d, w)


def _proj_rope_partial_kernel(x_ref, w_ref, c_ref, s1_ref, s2_ref, o_ref, *, half, scale):
    x = x_ref[...].astype(BF16)
    acc = jnp.dot(x, w_ref[...], preferred_element_type=F32)
    tn = acc.shape[1]
    reps = tn // LANES
    c = jnp.tile(c_ref[...], (1, reps))
    s1 = jnp.tile(s1_ref[...], (1, reps))
    s2 = jnp.tile(s2_ref[...], (1, reps))
    out = acc * c + pltpu.roll(acc, tn - half, 1) * s1 + pltpu.roll(acc, half, 1) * s2
    o_ref[...] = (out * scale).astype(o_ref.dtype)


def _proj_rope_partial(x2d, w, tabs, seq, scale, tm, tn):
    m, k = x2d.shape
    n = w.shape[1]
    c, s1, s2 = tabs
    half = ROPE_DIMS // 2
    tpos = seq // tm
    tab_spec = pl.BlockSpec((tm, LANES), lambda i, j: (i % tpos, 0))
    return pl.pallas_call(
        functools.partial(_proj_rope_partial_kernel, half=half, scale=scale),
        grid=(m // tm, n // tn),
        in_specs=[pl.BlockSpec((tm, k), lambda i, j: (i, 0)),
                  pl.BlockSpec((k, tn), lambda i, j: (0, j)),
                  tab_spec, tab_spec, tab_spec],
        out_specs=pl.BlockSpec((tm, tn), lambda i, j: (i, j)),
        out_shape=jax.ShapeDtypeStruct((m, n), BF16),
        compiler_params=_params("parallel", "parallel"),
        name="proj_rope_partial",
    )(x2d, w, c, s1, s2)


def _proj_rope_full_kernel(x_ref, w_ref, cos_ref, sin_ref, o_ref, *, dk, scale):
    x = x_ref[...].astype(BF16)
    acc = jnp.dot(x, w_ref[...], preferred_element_type=F32)
    cos = cos_ref[...]
    sin = sin_ref[...]
    half = dk // 2
    for hh in range(acc.shape[1] // dk):
        a = acc[:, hh * dk: hh * dk + half]
        b = acc[:, hh * dk + half: (hh + 1) * dk]
        o_ref[:, hh * dk: hh * dk + half] = ((a * cos - b * sin) * scale).astype(o_ref.dtype)
        o_ref[:, hh * dk + half: (hh + 1) * dk] = ((a * sin + b * cos) * scale).astype(o_ref.dtype)


def _proj_rope_full(x2d, w, cos, sin, seq, dk, scale, tm, tn):
    m, k = x2d.shape
    n = w.shape[1]
    tpos = seq // tm
    tab_spec = pl.BlockSpec((tm, dk // 2), lambda i, j: (i % tpos, 0))
    return pl.pallas_call(
        functools.partial(_proj_rope_full_kernel, dk=dk, scale=scale),
        grid=(m // tm, n // tn),
        in_specs=[pl.BlockSpec((tm, k), lambda i, j: (i, 0)),
                  pl.BlockSpec((k, tn), lambda i, j: (0, j)),
                  tab_spec, tab_spec],
        out_specs=pl.BlockSpec((tm, tn), lambda i, j: (i, j)),
        out_shape=jax.ShapeDtypeStruct((m, n), BF16),
        compiler_params=_params("parallel", "parallel"),
        name="proj_rope_full",
    )(x2d, w, cos, sin)


def _layer_norm(y, g, b):
    mu = jnp.mean(y, axis=-1, keepdims=True)
    d = y - mu
    var = jnp.mean(d * d, axis=-1, keepdims=True)
    return d * lax.rsqrt(var + LN_EPS) * g + b


def _outproj_ln_kernel(o_ref, w_ref, h_ref, g_ref, b_ref, out_ref, *, alpha):
    mix = jnp.dot(o_ref[...], w_ref[...], preferred_element_type=F32)
    y = alpha * h_ref[...] + mix
    out_ref[...] = _layer_norm(y, g_ref[...], b_ref[...])


def _outproj_ln(o2d, w, h2d, g, b, alpha, tm):
    m, kin = o2d.shape
    d = w.shape[1]
    return pl.pallas_call(
        functools.partial(_outproj_ln_kernel, alpha=alpha),
        grid=(m // tm,),
        in_specs=[pl.BlockSpec((tm, kin), lambda i: (i, 0)),
                  pl.BlockSpec((kin, d), lambda i: (0, 0)),
                  pl.BlockSpec((tm, d), lambda i: (i, 0)),
                  pl.BlockSpec((1, d), lambda i: (0, 0)),
                  pl.BlockSpec((1, d), lambda i: (0, 0))],
        out_specs=pl.BlockSpec((tm, d), lambda i: (i, 0)),
        out_shape=jax.ShapeDtypeStruct((m, d), F32),
        compiler_params=_params("parallel"),
        name="outproj_ln",
    )(o2d, w, h2d, g.reshape(1, d), b.reshape(1, d))


def _ffn_ln_kernel(h_ref, w13_ref, w2_ref, g_ref, b_ref, out_ref, *, alpha, dff, fc):
    h = h_ref[...]
    xb = h.astype(BF16)
    acc = alpha * h
    for c in range(dff // fc):
        gate = jnp.dot(xb, w13_ref[:, c * fc:(c + 1) * fc], preferred_element_type=F32)
        up = jnp.dot(xb, w13_ref[:, dff + c * fc: dff + (c + 1) * fc], preferred_element_type=F32)
        act = (_silu(gate) * up).astype(BF16)
        acc = acc + jnp.dot(act, w2_ref[c * fc:(c + 1) * fc, :], preferred_element_type=F32)
    out_ref[...] = _layer_norm(acc, g_ref[...], b_ref[...])


def _ffn_ln(h2d, w13, w2, g, b, alpha, tm, fc):
    m, d = h2d.shape
    dff = w2.shape[0]
    once = pl.Buffered(1)
    return pl.pallas_call(
        functools.partial(_ffn_ln_kernel, alpha=alpha, dff=dff, fc=fc),
        grid=(m // tm,),
        in_specs=[pl.BlockSpec((tm, d), lambda i: (i, 0)),
                  pl.BlockSpec((d, 2 * dff), lambda i: (0, 0), pipeline_mode=once),
                  pl.BlockSpec((dff, d), lambda i: (0, 0), pipeline_mode=once),
                  pl.BlockSpec((1, d), lambda i: (0, 0)),
                  pl.BlockSpec((1, d), lambda i: (0, 0))],
        out_specs=pl.BlockSpec((tm, d), lambda i: (i, 0)),
        out_shape=jax.ShapeDtypeStruct((m, d), F32),
        compiler_params=_params("parallel"),
        name="ffn_ln",
    )(h2d, w13, w2, g.reshape(1, d), b.reshape(1, d))


def _gdn_kernel(q_ref, k_ref, v_ref, z_ref, qp_ref, kp_ref, vp_ref, gates_ref,
                wq_ref, wk_ref, wv_ref, alog_ref, dtb_ref, ng_ref, o_ref, state_ref,
                *, tt, nheads):
    hd = pl.program_id(1)
    t = pl.program_id(2)
    c = GDN_CHUNK
    sc = GDN_SUPER
    ns = tt // sc
    nc = tt // c
    dk = GDN_DK

    @pl.when(t == 0)
    def _():
        state_ref[...] = jnp.zeros_like(state_ref)

    row16 = lax.broadcasted_iota(jnp.int32, (16, LANES), 0)

    def conv_silu(x_ref, xp_ref, w_ref):
        x = x_ref[0].astype(F32)
        xp = jnp.where(t > 0, xp_ref[0].astype(F32), 0.0)
        w = w_ref[...]
        y = x * w[GDN_CONV - 1:GDN_CONV, :]
        for s in range(1, GDN_CONV):
            xs = pltpu.roll(x, s, 0)
            head = jnp.where(row16 < s, pltpu.roll(xp, s, 0), xs[:16])
            xs = jnp.concatenate([head, xs[16:]], axis=0)
            y = y + xs * w[GDN_CONV - 1 - s:GDN_CONV - s, :]
        return _silu(y)

    def l2n(x):
        return x * lax.rsqrt(jnp.sum(x * x, axis=-1, keepdims=True) + NORM_EPS)

    qn = l2n(conv_silu(q_ref, qp_ref, wq_ref)) * (dk ** -0.5)
    kn = l2n(conv_silu(k_ref, kp_ref, wk_ref))
    vv = conv_silu(v_ref, vp_ref, wv_ref)

    gt = gates_ref[0]
    lane = lax.broadcasted_iota(jnp.int32, (tt, LANES), 1)
    rowp = lax.broadcasted_iota(jnp.int32, (tt, LANES), 0) % c
    sp_in = gt + dtb_ref[...]
    softplus = jnp.maximum(sp_in, 0.0) + jnp.log1p(jnp.exp(-jnp.abs(sp_in)))
    gc = -jnp.exp(alog_ref[...]) * softplus
    s = 1
    while s < c:
        gc = gc + jnp.where(rowp >= s, pltpu.roll(gc, s, 0), 0.0)
        s *= 2
    beta = jnp.sum(jnp.where(lane == hd, jax.nn.sigmoid(gt), 0.0), axis=-1, keepdims=True)
    gcol = jnp.sum(jnp.where(lane == nheads + hd, gc, 0.0), axis=-1, keepdims=True)

    glast = jnp.broadcast_to(gcol.reshape(nc, c, 1)[:, c - 1:c, :], (nc, c, 1)).reshape(tt, 1)
    eg = jnp.exp(gcol)
    kb = kn * beta
    kbg = kb * eg
    qd = qn * eg
    kd = kn * jnp.exp(glast - gcol)
    vb = vv * beta

    ri = lax.broadcasted_iota(jnp.int32, (sc, sc), 0)
    ci = lax.broadcasted_iota(jnp.int32, (sc, sc), 1)
    same = (ri // c) == (ci // c)
    tril = same & (ri >= ci)
    strict = same & (ri > ci)
    eye = (ri == ci).astype(F32)

    o_parts = []
    for sidx in range(ns):
        r0 = sidx * sc
        g_s = gcol[r0:r0 + sc]
        gb = jnp.broadcast_to(g_s, (sc, sc))
        diff = gb - gb.T
        dm = jnp.exp(jnp.where(tril, diff, NEG_INF))
        k_s = kn[r0:r0 + sc].astype(BF16)
        lhs = jnp.concatenate([kb[r0:r0 + sc], qn[r0:r0 + sc]], axis=0).astype(BF16)
        kq = lax.dot_general(lhs, k_s, NT_DIMS, preferred_element_type=F32)
        low = jnp.where(strict, kq[:sc] * dm, 0.0)
        attn = kq[sc:] * dm
        pw = low.astype(BF16)
        tinv = eye - low
        span = 2
        while span < c:
            pw_f = jnp.dot(pw, pw, preferred_element_type=F32)
            pw = pw_f.astype(BF16)
            tinv = tinv + jnp.dot(tinv.astype(BF16), pw, preferred_element_type=F32)
            span *= 2
        rhs = jnp.concatenate([vb[r0:r0 + sc], kbg[r0:r0 + sc]], axis=1).astype(BF16)
        uw = jnp.dot(tinv.astype(BF16), rhs, preferred_element_type=F32)
        uw_b = uw.astype(BF16)
        auw = jnp.dot(attn.astype(BF16), uw_b, preferred_element_type=F32)
        o_loc = auw[:, :dk]
        q_eff = (qd[r0:r0 + sc] - auw[:, dk:]).astype(BF16)
        kd_s = kd[r0:r0 + sc].astype(BF16)
        for half in range(sc // c):
            a0 = half * c
            kuw = lax.dot_general(kd_s[a0:a0 + c], uw_b[a0:a0 + c], TN_DIMS,
                                  preferred_element_type=F32)
            cd = jnp.exp(glast[r0 + a0:r0 + a0 + 1])
            st = state_ref[...]
            lhs2 = jnp.concatenate([kuw[:, dk:].astype(BF16), q_eff[a0:a0 + c]], axis=0)
            res = jnp.dot(lhs2, st.astype(BF16), preferred_element_type=F32)
            o_parts.append(o_loc[a0:a0 + c] + res[dk:])
            state_ref[...] = st * cd + kuw[:, :dk] - res[:dk]

    o = jnp.concatenate(o_parts, axis=0)
    o = o * lax.rsqrt(jnp.mean(o * o, axis=-1, keepdims=True) + NORM_EPS) * ng_ref[...]
    o = o * _silu(z_ref[0].astype(F32))
    o_ref[0] = o.astype(o_ref.dtype)


def _gdn_core(qkvz, gates, w_conv, a_log, dt_bias, norm_g, tt):
    bsz, seq, width = qkvz.shape
    nh = width // (4 * GDN_DK)
    pad = LANES - 2 * nh
    alog_row = jnp.concatenate([jnp.zeros((nh,), F32), a_log.astype(F32), jnp.zeros((pad,), F32)]).reshape(1, LANES)
    dtb_row = jnp.concatenate([jnp.zeros((nh,), F32), dt_bias.astype(F32), jnp.zeros((pad,), F32)]).reshape(1, LANES)
    prev = tt // 16

    def cur(off):
        return pl.BlockSpec((1, tt, LANES), lambda b, h, t: (b, t, off + h))

    def prv(off):
        return pl.BlockSpec((1, 16, LANES), lambda b, h, t: (b, jnp.maximum(t * prev - 1, 0), off + h))

    def cw(off):
        return pl.BlockSpec((GDN_CONV, LANES), lambda b, h, t: (0, off + h))

    row = pl.BlockSpec((1, LANES), lambda b, h, t: (0, 0))
    return pl.pallas_call(
        functools.partial(_gdn_kernel, tt=tt, nheads=nh),
        grid=(bsz, nh, seq // tt),
        in_specs=[cur(0), cur(nh), cur(2 * nh), cur(3 * nh),
                  prv(0), prv(nh), prv(2 * nh),
                  pl.BlockSpec((1, tt, LANES), lambda b, h, t: (b, t, 0)),
                  cw(0), cw(nh), cw(2 * nh), row, row, row],
        out_specs=pl.BlockSpec((1, tt, LANES), lambda b, h, t: (b, t, h)),
        out_shape=jax.ShapeDtypeStruct((bsz, seq, nh * GDN_DK), BF16),
        scratch_shapes=[pltpu.VMEM((GDN_DK, GDN_DK), F32)],
        compiler_params=_params("parallel", "parallel", "arbitrary"),
        name="gdn_core",
    )(qkvz, qkvz, qkvz, qkvz, qkvz, qkvz, qkvz, gates,
      w_conv, w_conv, w_conv, alog_row, dtb_row, norm_g.astype(F32).reshape(1, LANES))


def _gdn_layer(h2d, bsz, seq, w_in, w_conv, a_log, dt_bias, norm_g, w_out):
    nh = a_log.shape[0]
    main = 4 * nh * GDN_DK
    w_main = w_in[:, :main].astype(BF16)
    w_gate = jnp.pad(w_in[:, main:], ((0, 0), (0, LANES - 2 * nh))).astype(BF16)
    qkvz = _proj(h2d, w_main, BF16, 1024, 1024).reshape(bsz, seq, main)
    gates = _proj(h2d, w_gate, F32, 1024, LANES).reshape(bsz, seq, LANES)
    o = _gdn_core(qkvz, gates, w_conv.astype(F32), a_log, dt_bias, norm_g, tt=512)
    return o.reshape(bsz * seq, nh * GDN_DK), w_out.astype(BF16)


def _moba_kernel(q_ref, k_ref, v_ref, o_ref, kmh_ref, kml_ref, vt_ref, bias_ref, *, nb, bs):
    i = pl.program_id(2)
    dh = MOBA_DH

    @pl.when(i == 0)
    def _():
        kf = k_ref[0].astype(F32).reshape(nb, bs, dh)
        km = jnp.sum(kf, axis=1) * (1.0 / bs)
        hi = km.astype(BF16)
        kmh_ref[...] = hi
        kml_ref[...] = (km - hi.astype(F32)).astype(BF16)

        def tr(j, carry):
            vj = v_ref[0, pl.ds(pl.multiple_of(j * bs, bs), bs), :].astype(F32)
            vt_ref[j] = vj.T.astype(BF16)
            return carry
        lax.fori_loop(0, nb, tr, 0)

    q = q_ref[0]
    gate = (lax.dot_general(kmh_ref[...], q, NT_DIMS, preferred_element_type=F32) +
            lax.dot_general(kml_ref[...], q, NT_DIMS, preferred_element_type=F32))
    blk = lax.broadcasted_iota(jnp.int32, (nb, bs), 0)
    past = blk < i
    g = jnp.where(past, gate, -jnp.inf)
    sel = jnp.zeros((nb, bs), F32)
    for _ in range(MOBA_TOPK):
        mx = jnp.max(g, axis=0, keepdims=True)
        idx = jnp.min(jnp.where(g == mx, blk, nb), axis=0, keepdims=True)
        hit = blk == idx
        sel = jnp.where(hit, 1.0, sel)
        g = jnp.where(hit, -jnp.inf, g)
    bias_ref[...] = jnp.where(past & (sel > 0.0), 0.0, NEG_INF)

    own = pl.multiple_of(i * bs, bs)
    k_own = k_ref[0, pl.ds(own, bs), :]
    s = lax.dot_general(k_own, q, NT_DIMS, preferred_element_type=F32)
    kpos = lax.broadcasted_iota(jnp.int32, (bs, bs), 0)
    qpos = lax.broadcasted_iota(jnp.int32, (bs, bs), 1)
    s = jnp.where(kpos <= qpos, s, NEG_INF)
    m0 = jnp.max(s, axis=0, keepdims=True)
    p = jnp.exp(s - m0)
    l0 = jnp.sum(p, axis=0, keepdims=True)
    acc0 = jnp.dot(vt_ref[i], p.astype(BF16), preferred_element_type=F32)

    def body(j, carry):
        m, l, acc = carry
        kj = k_ref[0, pl.ds(pl.multiple_of(j * bs, bs), bs), :]
        sj = lax.dot_general(kj, q, NT_DIMS, preferred_element_type=F32) + bias_ref[pl.ds(j, 1), :]
        m_new = jnp.maximum(m, jnp.max(sj, axis=0, keepdims=True))
        corr = jnp.exp(m - m_new)
        pj = jnp.exp(sj - m_new)
        l = corr * l + jnp.sum(pj, axis=0, keepdims=True)
        acc = corr * acc + jnp.dot(vt_ref[j], pj.astype(BF16), preferred_element_type=F32)
        return m_new, l, acc

    _, l, acc = lax.fori_loop(0, i, body, (m0, l0, acc0))
    o_ref[0] = (acc / l).T.astype(o_ref.dtype)


def _moba_core(q, k, v):
    bsz, seq, width = q.shape
    nh = width // MOBA_DH
    bs = MOBA_BLOCK
    nb = seq // bs
    full = pl.BlockSpec((1, seq, MOBA_DH), lambda b, h, i: (b, 0, h))
    blk = pl.BlockSpec((1, bs, MOBA_DH), lambda b, h, i: (b, i, h))
    return pl.pallas_call(
        functools.partial(_moba_kernel, nb=nb, bs=bs),
        grid=(bsz, nh, nb),
        in_specs=[blk, full, full],
        out_specs=blk,
        out_shape=jax.ShapeDtypeStruct((bsz, seq, width), BF16),
        scratch_shapes=[pltpu.VMEM((nb, MOBA_DH), BF16), pltpu.VMEM((nb, MOBA_DH), BF16),
                        pltpu.VMEM((nb, MOBA_DH, bs), BF16), pltpu.VMEM((nb, bs), F32)],
        compiler_params=_params("parallel", "parallel", "arbitrary"),
        name="moba_core",
    )(q, k, v)


def _moba_rope_tables(seq):
    half = ROPE_DIMS // 2
    inv_freq = jnp.power(ROPE_THETA, -jnp.arange(half, dtype=F32) / half)
    ang = jnp.arange(seq, dtype=F32)[:, None] * inv_freq[None, :]
    cos, sin = jnp.cos(ang), jnp.sin(ang)
    rest = LANES - 2 * half
    c = jnp.concatenate([cos, cos, jnp.ones((seq, rest), F32)], axis=1)
    s1 = jnp.concatenate([-sin, jnp.zeros((seq, LANES - half), F32)], axis=1)
    s2 = jnp.concatenate([jnp.zeros((seq, half), F32), sin, jnp.zeros((seq, rest), F32)], axis=1)
    return c, s1, s2


def _moba_layer(h2d, bsz, seq, w_qkv, w_out):
    width = w_qkv.shape[1] // 3
    nh = width // MOBA_DH
    assert seq % MOBA_BLOCK == 0
    tabs = _moba_rope_tables(seq)
    wq, wk, wv = (w_qkv[:, n * width:(n + 1) * width].astype(BF16) for n in range(3))
    q = _proj_rope_partial(h2d, wq, tabs, seq, MOBA_DH ** -0.5, 1024, width)
    k = _proj_rope_partial(h2d, wk, tabs, seq, 1.0, 1024, width)
    v = _proj(h2d, wv, BF16, 1024, width)
    shp = (bsz, seq, width)
    o = _moba_core(q.reshape(shp), k.reshape(shp), v.reshape(shp))
    return o.reshape(bsz * seq, nh * MOBA_DH), w_out.astype(BF16)


def _ret_kernel(lg_ref, q_ref, k_ref, v_ref, gate_ref, gn_ref, o_ref, state_ref, *, c):
    t = pl.program_id(2)

    @pl.when(t == 0)
    def _():
        state_ref[...] = jnp.zeros_like(state_ref)

    lg = lg_ref[0][:, :1]
    q = q_ref[0]
    k = k_ref[0]
    v = v_ref[0]
    ri = lax.broadcasted_iota(jnp.int32, (c, c), 0)
    ci = lax.broadcasted_iota(jnp.int32, (c, c), 1)
    tril = ri >= ci
    dmat = jnp.where(tril, jnp.exp(lg * jnp.where(tril, ri - ci, 0).astype(F32)), 0.0)
    inner = lax.dot_general(q, k, NT_DIMS, preferred_element_type=F32) * dmat
    o = jnp.dot(inner.astype(BF16), v, preferred_element_type=F32)
    pos = lax.broadcasted_iota(jnp.int32, (c, 1), 0).astype(F32)
    st = state_ref[...]
    qd = (q.astype(F32) * jnp.exp(lg * (pos + 1.0))).astype(BF16)
    o = o + jnp.dot(qd, st.astype(BF16), preferred_element_type=F32)
    kd = (k.astype(F32) * jnp.exp(lg * (c - 1.0 - pos))).astype(BF16)
    state_ref[...] = st * jnp.exp(lg * c) + lax.dot_general(kd, v, TN_DIMS, preferred_element_type=F32)

    mu = jnp.mean(o, axis=-1, keepdims=True)
    d = o - mu
    var = jnp.mean(d * d, axis=-1, keepdims=True)
    o = d * lax.rsqrt(var + LN_EPS) * gn_ref[...]
    o_ref[0] = (_silu(gate_ref[0].astype(F32)) * o).astype(o_ref.dtype)


def _ret_core(q, k, v, gate, gn_g, log_gamma):
    bsz, seq, _ = q.shape
    nh = log_gamma.shape[0]
    c = RET_TILE
    lg = jnp.broadcast_to(log_gamma.reshape(nh, 1, 1), (nh, 1, LANES)).astype(F32)
    qk_spec = pl.BlockSpec((1, c, RET_DK), lambda b, h, t: (b, t, h))
    v_spec = pl.BlockSpec((1, c, RET_DV), lambda b, h, t: (b, t, h))
    return pl.pallas_call(
        functools.partial(_ret_kernel, c=c),
        grid=(bsz, nh, seq // c),
        in_specs=[pl.BlockSpec((1, 1, LANES), lambda b, h, t: (h, 0, 0)),
                  qk_spec, qk_spec, v_spec, v_spec,
                  pl.BlockSpec((1, RET_DV), lambda b, h, t: (0, h))],
        out_specs=v_spec,
        out_shape=jax.ShapeDtypeStruct((bsz, seq, nh * RET_DV), BF16),
        scratch_shapes=[pltpu.VMEM((RET_DK, RET_DV), F32)],
        compiler_params=_params("parallel", "parallel", "arbitrary"),
        name="ret_core",
    )(lg, q, k, v, gate, gn_g.astype(F32).reshape(1, nh * RET_DV))


def _ret_layer(h2d, bsz, seq, w_in, gn_g, w_out):
    nh = gn_g.shape[0] // RET_DV
    qw = nh * RET_DK
    vw = nh * RET_DV
    assert seq % RET_TILE == 0
    inv_freq = jnp.power(XPOS_BASE, -jnp.linspace(0.0, 1.0, RET_DK // 2, dtype=F32))
    ang = jnp.arange(seq, dtype=F32)[:, None] * inv_freq[None, :]
    cos, sin = jnp.cos(ang), jnp.sin(ang)
    log_gamma = jnp.log1p(-jnp.exp2(-5.0 - jnp.arange(nh, dtype=F32)))
    wb = w_in.astype(BF16)
    q = _proj_rope_full(h2d, wb[:, :qw], cos, sin, seq, RET_DK, 1.0, 1024, qw)
    k = _proj_rope_full(h2d, wb[:, qw:2 * qw], cos, sin, seq, RET_DK, RET_DK ** -0.5, 1024, qw)
    v = _proj(h2d, wb[:, 2 * qw:2 * qw + vw], BF16, 1024, 1024)
    gate = _proj(h2d, wb[:, 2 * qw + vw:], BF16, 1024, 1024)
    o = _ret_core(q.reshape(bsz, seq, qw), k.reshape(bsz, seq, qw), v.reshape(bsz, seq, vw),
                  gate.reshape(bsz, seq, vw), gn_g, log_gamma)
    return o.reshape(bsz * seq, vw), w_out.astype(BF16)


def kernel(x, a_w_in, a_conv, a_a_log, a_dt_bias, a_norm_g, a_w_out, b_w_qkv, b_w_out,
           c_w_in, c_gn_g, c_w_out, f_w13, f_w2, ln1_g, ln1_b, ln2_g, ln2_b):
    bsz, seq, d = x.shape
    depth = f_w13.shape[0]
    alpha = (2 * depth) ** 0.25
    h = x.reshape(bsz * seq, d)
    for i in range(depth):
        kind, j = i % N_MIXERS, i // N_MIXERS
        if kind == 0:
            o, w_out = _gdn_layer(h, bsz, seq, a_w_in[j], a_conv[j], a_a_log[j], a_dt_bias[j],
                                  a_norm_g[j], a_w_out[j])
        elif kind == 1:
            o, w_out = _moba_layer(h, bsz, seq, b_w_qkv[j], b_w_out[j])
        else:
            o, w_out = _ret_layer(h, bsz, seq, c_w_in[j], c_gn_g[j], c_w_out[j])
        h = _outproj_ln(o, w_out, h, ln1_g[i], ln1_b[i], alpha, 512)
        h = _ffn_ln(h, f_w13[i].astype(BF16), f_w2[i].astype(BF16), ln2_g[i], ln2_b[i], alpha, 512, 256)
    return h.reshape(bsz, seq, d)
```

```python
import functools
import math

import jax
import jax.numpy as jnp
from jax import lax
from jax.experimental import pallas as pl
from jax.experimental.pallas import tpu as pltpu

F32 = jnp.float32
BF16 = jnp.bfloat16

N_MIXERS = 3
LANES = 128
VMEM_LIMIT = 56 * 1024 * 1024

GDN_DK = 128
GDN_CHUNK = 64
GDN_SUPER = 128
GDN_CONV = 4
MOBA_DH = 128
MOBA_BLOCK = 256
MOBA_TOPK = 3
MOBA_GROUP = 8
ONES_ROWS = 16
ROPE_THETA = 500000.0
ROPE_DIMS = MOBA_DH // 4
RET_DK = 256
RET_DV = 512
RET_TILE = 256
XPOS_BASE = 10000.0
LN_EPS = 1e-5
NORM_EPS = 1e-6
NEG_INF = -1e30

NT_DIMS = (((1,), (1,)), ((), ()))
TN_DIMS = (((0,), (0,)), ((), ()))


def _params(*semantics):
    return pltpu.CompilerParams(dimension_semantics=semantics, vmem_limit_bytes=VMEM_LIMIT)


def _silu(x):
    return x * jax.nn.sigmoid(x)


def _proj_kernel(x_ref, w_ref, o_ref):
    x = x_ref[...].astype(BF16)
    o_ref[...] = jnp.dot(x, w_ref[...], preferred_element_type=F32).astype(o_ref.dtype)


def _proj(x2d, w, out_dtype, tm, tn):
    m, k = x2d.shape
    n = w.shape[1]
    return pl.pallas_call(
        _proj_kernel,
        grid=(m // tm, n // tn),
        in_specs=[pl.BlockSpec((tm, k), lambda i, j: (i, 0)),
                  pl.BlockSpec((k, tn), lambda i, j: (0, j))],
        out_specs=pl.BlockSpec((tm, tn), lambda i, j: (i, j)),
        out_shape=jax.ShapeDtypeStruct((m, n), out_dtype),
        compiler_params=_params("parallel", "parallel"),
        name="proj",
    )(x2d, w)


def _proj_rope_partial_kernel(x_ref, w_ref, c_ref, s1_ref, s2_ref, o_ref, *, half, scale):
    x = x_ref[...].astype(BF16)
    acc = jnp.dot(x, w_ref[...], preferred_element_type=F32)
    tn = acc.shape[1]
    reps = tn // LANES
    c = jnp.tile(c_ref[...], (1, reps))
    s1 = jnp.tile(s1_ref[...], (1, reps))
    s2 = jnp.tile(s2_ref[...], (1, reps))
    out = acc * c + pltpu.roll(acc, tn - half, 1) * s1 + pltpu.roll(acc, half, 1) * s2
    o_ref[...] = (out * scale).astype(o_ref.dtype)


def _proj_rope_partial(x2d, w, tabs, seq, scale, tm, tn):
    m, k = x2d.shape
    n = w.shape[1]
    c, s1, s2 = tabs
    half = ROPE_DIMS // 2
    tpos = seq // tm
    tab_spec = pl.BlockSpec((tm, LANES), lambda i, j: (i % tpos, 0))
    return pl.pallas_call(
        functools.partial(_proj_rope_partial_kernel, half=half, scale=scale),
        grid=(m // tm, n // tn),
        in_specs=[pl.BlockSpec((tm, k), lambda i, j: (i, 0)),
                  pl.BlockSpec((k, tn), lambda i, j: (0, j)),
                  tab_spec, tab_spec, tab_spec],
        out_specs=pl.BlockSpec((tm, tn), lambda i, j: (i, j)),
        out_shape=jax.ShapeDtypeStruct((m, n), BF16),
        compiler_params=_params("parallel", "parallel"),
        name="proj_rope_partial",
    )(x2d, w, c, s1, s2)


def _proj_rope_full_kernel(x_ref, w_ref, cos_ref, sin_ref, o_ref, *, dk, scale):
    x = x_ref[...].astype(BF16)
    acc = jnp.dot(x, w_ref[...], preferred_element_type=F32)
    cos = cos_ref[...]
    sin = sin_ref[...]
    half = dk // 2
    for hh in range(acc.shape[1] // dk):
        a = acc[:, hh * dk: hh * dk + half]
        b = acc[:, hh * dk + half: (hh + 1) * dk]
        o_ref[:, hh * dk: hh * dk + half] = ((a * cos - b * sin) * scale).astype(o_ref.dtype)
        o_ref[:, hh * dk + half: (hh + 1) * dk] = ((a * sin + b * cos) * scale).astype(o_ref.dtype)


def _proj_rope_full(x2d, w, cos, sin, seq, dk, scale, tm, tn):
    m, k = x2d.shape
    n = w.shape[1]
    tpos = seq // tm
    tab_spec = pl.BlockSpec((tm, dk // 2), lambda i, j: (i % tpos, 0))
    return pl.pallas_call(
        functools.partial(_proj_rope_full_kernel, dk=dk, scale=scale),
        grid=(m // tm, n // tn),
        in_specs=[pl.BlockSpec((tm, k), lambda i, j: (i, 0)),
                  pl.BlockSpec((k, tn), lambda i, j: (0, j)),
                  tab_spec, tab_spec],
        out_specs=pl.BlockSpec((tm, tn), lambda i, j: (i, j)),
        out_shape=jax.ShapeDtypeStruct((m, n), BF16),
        compiler_params=_params("parallel", "parallel"),
        name="proj_rope_full",
    )(x2d, w, cos, sin)


def _layer_norm(y, g, b):
    mu = jnp.mean(y, axis=-1, keepdims=True)
    d = y - mu
    var = jnp.mean(d * d, axis=-1, keepdims=True)
    return d * lax.rsqrt(var + LN_EPS) * g + b


def _outproj_ln_kernel(o_ref, w_ref, h_ref, g_ref, b_ref, out_ref, *, alpha):
    mix = jnp.dot(o_ref[...], w_ref[...], preferred_element_type=F32)
    y = alpha * h_ref[...] + mix
    out_ref[...] = _layer_norm(y, g_ref[...], b_ref[...])


def _outproj_ln(o2d, w, h2d, g, b, alpha, tm):
    m, kin = o2d.shape
    d = w.shape[1]
    return pl.pallas_call(
        functools.partial(_outproj_ln_kernel, alpha=alpha),
        grid=(m // tm,),
        in_specs=[pl.BlockSpec((tm, kin), lambda i: (i, 0)),
                  pl.BlockSpec((kin, d), lambda i: (0, 0)),
                  pl.BlockSpec((tm, d), lambda i: (i, 0)),
                  pl.BlockSpec((1, d), lambda i: (0, 0)),
                  pl.BlockSpec((1, d), lambda i: (0, 0))],
        out_specs=pl.BlockSpec((tm, d), lambda i: (i, 0)),
        out_shape=jax.ShapeDtypeStruct((m, d), F32),
        compiler_params=_params("parallel"),
        name="outproj_ln",
    )(o2d, w, h2d, g.reshape(1, d), b.reshape(1, d))


def _ffn_ln_kernel(h_ref, w13_ref, w2_ref, g_ref, b_ref, out_ref, *, alpha, dff, fc):
    h = h_ref[...]
    xb = h.astype(BF16)
    acc = alpha * h
    for c in range(dff // fc):
        gate = jnp.dot(xb, w13_ref[:, c * fc:(c + 1) * fc], preferred_element_type=F32)
        up = jnp.dot(xb, w13_ref[:, dff + c * fc: dff + (c + 1) * fc], preferred_element_type=F32)
        act = (_silu(gate) * up).astype(BF16)
        acc = acc + jnp.dot(act, w2_ref[c * fc:(c + 1) * fc, :], preferred_element_type=F32)
    out_ref[...] = _layer_norm(acc, g_ref[...], b_ref[...])


def _ffn_ln(h2d, w13, w2, g, b, alpha, tm, fc):
    m, d = h2d.shape
    dff = w2.shape[0]
    once = pl.Buffered(1)
    return pl.pallas_call(
        functools.partial(_ffn_ln_kernel, alpha=alpha, dff=dff, fc=fc),
        grid=(m // tm,),
        in_specs=[pl.BlockSpec((tm, d), lambda i: (i, 0)),
                  pl.BlockSpec((d, 2 * dff), lambda i: (0, 0), pipeline_mode=once),
                  pl.BlockSpec((dff, d), lambda i: (0, 0), pipeline_mode=once),
                  pl.BlockSpec((1, d), lambda i: (0, 0)),
                  pl.BlockSpec((1, d), lambda i: (0, 0))],
        out_specs=pl.BlockSpec((tm, d), lambda i: (i, 0)),
        out_shape=jax.ShapeDtypeStruct((m, d), F32),
        compiler_params=_params("parallel"),
        name="ffn_ln",
    )(h2d, w13, w2, g.reshape(1, d), b.reshape(1, d))


def _gdn_kernel(q_ref, k_ref, v_ref, z_ref, qp_ref, kp_ref, vp_ref, gates_ref,
                wq_ref, wk_ref, wv_ref, alog_ref, dtb_ref, ng_ref, o_ref, state_ref,
                *, tt, nheads, hp):
    hg = pl.program_id(1)
    t = pl.program_id(2)
    c = GDN_CHUNK
    sc = GDN_SUPER
    ns = tt // sc
    nc = tt // c
    dk = GDN_DK

    @pl.when(t == 0)
    def _():
        state_ref[...] = jnp.zeros_like(state_ref)

    row16 = lax.broadcasted_iota(jnp.int32, (16, LANES), 0)

    def conv_silu(x, xp, w):
        x = x.astype(F32)
        xp = jnp.where(t > 0, xp.astype(F32), 0.0)
        y = x * w[GDN_CONV - 1:GDN_CONV, :]
        for s in range(1, GDN_CONV):
            xs = pltpu.roll(x, s, 0)
            head = jnp.where(row16 < s, pltpu.roll(xp, s, 0), xs[:16])
            xs = jnp.concatenate([head, xs[16:]], axis=0)
            y = y + xs * w[GDN_CONV - 1 - s:GDN_CONV - s, :]
        return _silu(y)

    def l2n(x):
        return x * lax.rsqrt(jnp.sum(x * x, axis=-1, keepdims=True) + NORM_EPS)

    gt = gates_ref[0]
    lane = lax.broadcasted_iota(jnp.int32, (tt, LANES), 1)
    rowp = lax.broadcasted_iota(jnp.int32, (tt, LANES), 0) % c
    sp_in = gt + dtb_ref[...]
    softplus = jnp.maximum(sp_in, 0.0) + jnp.log1p(jnp.exp(-jnp.abs(sp_in)))
    gc = -jnp.exp(alog_ref[...]) * softplus
    s = 1
    while s < c:
        gc = gc + jnp.where(rowp >= s, pltpu.roll(gc, s, 0), 0.0)
        s *= 2
    beta_all = jax.nn.sigmoid(gt)

    ri = lax.broadcasted_iota(jnp.int32, (sc, sc), 0)
    ci = lax.broadcasted_iota(jnp.int32, (sc, sc), 1)
    same = (ri // c) == (ci // c)
    tril = same & (ri >= ci)
    strict = same & (ri > ci)
    eye = (ri == ci).astype(F32)

    units = [(hh, sidx) for hh in range(hp) for sidx in range(ns)]
    dm, lhs, k_s, rhs, qd_u, kd_u, cd = {}, {}, {}, {}, {}, {}, {}
    for hh in range(hp):
        ln = slice(hh * dk, (hh + 1) * dk)
        hd = hg * hp + hh
        qn = l2n(conv_silu(q_ref[0, :, ln], qp_ref[0, :, ln], wq_ref[:, ln])) * (dk ** -0.5)
        kn = l2n(conv_silu(k_ref[0, :, ln], kp_ref[0, :, ln], wk_ref[:, ln]))
        vv = conv_silu(v_ref[0, :, ln], vp_ref[0, :, ln], wv_ref[:, ln])
        beta = jnp.sum(jnp.where(lane == hd, beta_all, 0.0), axis=-1, keepdims=True)
        gcol = jnp.sum(jnp.where(lane == nheads + hd, gc, 0.0), axis=-1, keepdims=True)
        glast = jnp.broadcast_to(gcol.reshape(nc, c, 1)[:, c - 1:c, :], (nc, c, 1)).reshape(tt, 1)
        eg = jnp.exp(gcol)
        kb = kn * beta
        kbg = kb * eg
        qd = qn * eg
        kd = kn * jnp.exp(glast - gcol)
        vb = vv * beta
        for sidx in range(ns):
            r = slice(sidx * sc, (sidx + 1) * sc)
            u = (hh, sidx)
            gb = jnp.broadcast_to(gcol[r], (sc, sc))
            dm[u] = jnp.exp(jnp.where(tril, gb - gb.T, NEG_INF))
            k_s[u] = kn[r].astype(BF16)
            lhs[u] = jnp.concatenate([kb[r], qn[r]], axis=0).astype(BF16)
            rhs[u] = jnp.concatenate([vb[r], kbg[r]], axis=1).astype(BF16)
            qd_u[u] = qd[r]
            kd_u[u] = kd[r].astype(BF16)
            for half in range(sc // c):
                row = sidx * sc + half * c
                cd[hh, sidx * (sc // c) + half] = jnp.exp(glast[row:row + 1])

    kq = {u: lax.dot_general(lhs[u], k_s[u], NT_DIMS, preferred_element_type=F32) for u in units}
    attn = {u: (kq[u][sc:] * dm[u]).astype(BF16) for u in units}
    low = {u: jnp.where(strict, kq[u][:sc] * dm[u], 0.0) for u in units}
    pw = {u: low[u].astype(BF16) for u in units}
    tinv = {u: eye - low[u] for u in units}
    span = 2
    while span < c:
        pw = {u: jnp.dot(pw[u], pw[u], preferred_element_type=F32).astype(BF16) for u in units}
        tinv = {u: tinv[u] + jnp.dot(tinv[u].astype(BF16), pw[u], preferred_element_type=F32) for u in units}
        span *= 2
    uw_b = {u: jnp.dot(tinv[u].astype(BF16), rhs[u], preferred_element_type=F32).astype(BF16)
            for u in units}
    auw = {u: jnp.dot(attn[u], uw_b[u], preferred_element_type=F32) for u in units}
    o_loc, q_eff, kuw = {}, {}, {}
    for u in units:
        hh, sidx = u
        qe = (qd_u[u] - auw[u][:, dk:]).astype(BF16)
        for half in range(sc // c):
            a = slice(half * c, (half + 1) * c)
            ch = sidx * (sc // c) + half
            o_loc[hh, ch] = auw[u][a, :dk]
            q_eff[hh, ch] = qe[a]
            kuw[hh, ch] = lax.dot_general(kd_u[u][a], uw_b[u][a], TN_DIMS,
                                          preferred_element_type=F32)

    o_parts = [[] for _ in range(hp)]
    for ch in range(nc):
        for hh in range(hp):
            st = state_ref[hh]
            lhs2 = jnp.concatenate([kuw[hh, ch][:, dk:].astype(BF16), q_eff[hh, ch]], axis=0)
            res = jnp.dot(lhs2, st.astype(BF16), preferred_element_type=F32)
            o_parts[hh].append(o_loc[hh, ch] + res[dk:])
            state_ref[hh] = st * cd[hh, ch] + kuw[hh, ch][:, :dk] - res[:dk]

    for hh in range(hp):
        ln = slice(hh * dk, (hh + 1) * dk)
        o = jnp.concatenate(o_parts[hh], axis=0)
        o = o * lax.rsqrt(jnp.mean(o * o, axis=-1, keepdims=True) + NORM_EPS) * ng_ref[...]
        o = o * _silu(z_ref[0, :, ln].astype(F32))
        o_ref[0, :, ln] = o.astype(o_ref.dtype)


def _gdn_core(qkvz, gates, w_conv, a_log, dt_bias, norm_g, tt, hp):
    bsz, seq, width = qkvz.shape
    nh = width // (4 * GDN_DK)
    ng = nh // hp
    hw = hp * GDN_DK
    pad = LANES - 2 * nh
    alog_row = jnp.concatenate([jnp.zeros((nh,), F32), a_log.astype(F32), jnp.zeros((pad,), F32)]).reshape(1, LANES)
    dtb_row = jnp.concatenate([jnp.zeros((nh,), F32), dt_bias.astype(F32), jnp.zeros((pad,), F32)]).reshape(1, LANES)
    prev = tt // 16

    def cur(off):
        return pl.BlockSpec((1, tt, hw), lambda b, h, t: (b, t, off + h))

    def prv(off):
        return pl.BlockSpec((1, 16, hw), lambda b, h, t: (b, jnp.maximum(t * prev - 1, 0), off + h))

    def cw(off):
        return pl.BlockSpec((GDN_CONV, hw), lambda b, h, t: (0, off + h))

    row = pl.BlockSpec((1, LANES), lambda b, h, t: (0, 0))
    return pl.pallas_call(
        functools.partial(_gdn_kernel, tt=tt, nheads=nh, hp=hp),
        grid=(bsz, ng, seq // tt),
        in_specs=[cur(0), cur(ng), cur(2 * ng), cur(3 * ng),
                  prv(0), prv(ng), prv(2 * ng),
                  pl.BlockSpec((1, tt, LANES), lambda b, h, t: (b, t, 0)),
                  cw(0), cw(ng), cw(2 * ng), row, row, row],
        out_specs=pl.BlockSpec((1, tt, hw), lambda b, h, t: (b, t, h)),
        out_shape=jax.ShapeDtypeStruct((bsz, seq, nh * GDN_DK), BF16),
        scratch_shapes=[pltpu.VMEM((hp, GDN_DK, GDN_DK), F32)],
        compiler_params=_params("parallel", "parallel", "arbitrary"),
        name="gdn_core",
    )(qkvz, qkvz, qkvz, qkvz, qkvz, qkvz, qkvz, gates,
      w_conv, w_conv, w_conv, alog_row, dtb_row, norm_g.astype(F32).reshape(1, LANES))


def _gdn_layer(h2d, bsz, seq, w_in, w_conv, a_log, dt_bias, norm_g, w_out):
    nh = a_log.shape[0]
    main = 4 * nh * GDN_DK
    w_main = w_in[:, :main].astype(BF16)
    w_gate = jnp.pad(w_in[:, main:], ((0, 0), (0, LANES - 2 * nh))).astype(BF16)
    qkvz = _proj(h2d, w_main, BF16, 1024, 1024).reshape(bsz, seq, main)
    gates = _proj(h2d, w_gate, F32, 1024, LANES).reshape(bsz, seq, LANES)
    o = _gdn_core(qkvz, gates, w_conv.astype(F32), a_log, dt_bias, norm_g, tt=256, hp=4)
    return o.reshape(bsz * seq, nh * GDN_DK), w_out.astype(BF16)


def _moba_kernel(q_ref, k_ref, v_ref, o_ref, kmh_ref, kml_ref, vt_ref, bias_ref, *, nb, bs, grp):
    i = pl.program_id(2)
    dh = MOBA_DH
    gw = grp * bs

    @pl.when(i == 0)
    def _():
        kf = k_ref[0].astype(F32).reshape(nb, bs, dh)
        km = jnp.sum(kf, axis=1) * (1.0 / bs)
        hi = km.astype(BF16)
        kmh_ref[...] = hi
        kml_ref[...] = (km - hi.astype(F32)).astype(BF16)

        def tr(j, carry):
            vj = v_ref[0, pl.ds(pl.multiple_of(j * gw, gw), gw), :].astype(F32)
            vt_ref[j, :dh, :] = vj.T.astype(BF16)
            vt_ref[j, dh:, :] = jnp.ones((ONES_ROWS, gw), BF16)
            return carry
        lax.fori_loop(0, nb // grp, tr, 0)

    q = q_ref[0]
    gate = (lax.dot_general(kmh_ref[...], q, NT_DIMS, preferred_element_type=F32) +
            lax.dot_general(kml_ref[...], q, NT_DIMS, preferred_element_type=F32))
    blk = lax.broadcasted_iota(jnp.int32, (nb, bs), 0)
    past = blk < i
    g = jnp.where(past, gate, -jnp.inf)
    sel = jnp.zeros((nb, bs), F32)
    for _ in range(MOBA_TOPK):
        mx = jnp.max(g, axis=0, keepdims=True)
        idx = jnp.min(jnp.where(g == mx, blk, nb), axis=0, keepdims=True)
        hit = blk == idx
        sel = jnp.where(hit, 1.0, sel)
        g = jnp.where(hit, -jnp.inf, g)
    bias_ref[...] = jnp.where(past & (sel > 0.0), 0.0, NEG_INF)

    own = pl.multiple_of(i * bs, bs)
    k_own = k_ref[0, pl.ds(own, bs), :]
    s = lax.dot_general(k_own, q, NT_DIMS, preferred_element_type=F32)
    kpos = lax.broadcasted_iota(jnp.int32, (bs, bs), 0)
    qpos = lax.broadcasted_iota(jnp.int32, (bs, bs), 1)
    s = jnp.where(kpos <= qpos, s, NEG_INF)
    m0 = jnp.max(s, axis=0, keepdims=True)
    p = jnp.exp2(s - m0)
    v_own = jnp.concatenate([v_ref[0, pl.ds(own, bs), :], jnp.ones((bs, ONES_ROWS), BF16)], axis=1)
    acc0 = lax.dot_general(v_own, p.astype(BF16), TN_DIMS, preferred_element_type=F32)

    def body(gi, carry):
        m, acc = carry
        kg = k_ref[0, pl.ds(pl.multiple_of(gi * gw, gw), gw), :]
        sg = lax.dot_general(kg, q, NT_DIMS, preferred_element_type=F32)
        bias = [bias_ref[pl.ds(gi * grp + u, 1), :] for u in range(grp)]
        m_new = m
        for u in range(grp):
            m_new = jnp.maximum(m_new, jnp.max(sg[u * bs:(u + 1) * bs], axis=0, keepdims=True) + bias[u])
        acc = jnp.exp2(m - m_new) * acc
        for u in range(grp):
            pu = jnp.exp2(sg[u * bs:(u + 1) * bs] - (m_new - bias[u])).astype(BF16)
            acc = acc + jnp.dot(vt_ref[gi, :, u * bs:(u + 1) * bs], pu, preferred_element_type=F32)
        return m_new, acc

    _, acc = lax.fori_loop(0, (i + grp - 1) // grp, body, (m0, acc0))
    o_ref[0] = (acc[:dh] / acc[dh:dh + 1]).T.astype(o_ref.dtype)


def _moba_core(q, k, v):
    bsz, seq, width = q.shape
    nh = width // MOBA_DH
    bs = MOBA_BLOCK
    nb = seq // bs
    grp = math.gcd(nb, MOBA_GROUP)
    full = pl.BlockSpec((1, seq, MOBA_DH), lambda b, h, i: (b, 0, h))
    blk = pl.BlockSpec((1, bs, MOBA_DH), lambda b, h, i: (b, i, h))
    return pl.pallas_call(
        functools.partial(_moba_kernel, nb=nb, bs=bs, grp=grp),
        grid=(bsz, nh, nb),
        in_specs=[blk, full, full],
        out_specs=blk,
        out_shape=jax.ShapeDtypeStruct((bsz, seq, width), BF16),
        scratch_shapes=[pltpu.VMEM((nb, MOBA_DH), BF16), pltpu.VMEM((nb, MOBA_DH), BF16),
                        pltpu.VMEM((nb // grp, MOBA_DH + ONES_ROWS, grp * bs), BF16),
                        pltpu.VMEM((nb, bs), F32)],
        compiler_params=_params("parallel", "parallel", "arbitrary"),
        name="moba_core",
    )(q, k, v)


def _moba_rope_tables(seq):
    half = ROPE_DIMS // 2
    inv_freq = jnp.power(ROPE_THETA, -jnp.arange(half, dtype=F32) / half)
    ang = jnp.arange(seq, dtype=F32)[:, None] * inv_freq[None, :]
    cos, sin = jnp.cos(ang), jnp.sin(ang)
    rest = LANES - 2 * half
    c = jnp.concatenate([cos, cos, jnp.ones((seq, rest), F32)], axis=1)
    s1 = jnp.concatenate([-sin, jnp.zeros((seq, LANES - half), F32)], axis=1)
    s2 = jnp.concatenate([jnp.zeros((seq, half), F32), sin, jnp.zeros((seq, rest), F32)], axis=1)
    return c, s1, s2


def _moba_layer(h2d, bsz, seq, w_qkv, w_out):
    width = w_qkv.shape[1] // 3
    nh = width // MOBA_DH
    assert seq % MOBA_BLOCK == 0
    tabs = _moba_rope_tables(seq)
    wq, wk, wv = (w_qkv[:, n * width:(n + 1) * width].astype(BF16) for n in range(3))
    q = _proj_rope_partial(h2d, wq, tabs, seq, MOBA_DH ** -0.5 * math.log2(math.e), 1024, width)
    k = _proj_rope_partial(h2d, wk, tabs, seq, 1.0, 1024, width)
    v = _proj(h2d, wv, BF16, 1024, width)
    shp = (bsz, seq, width)
    o = _moba_core(q.reshape(shp), k.reshape(shp), v.reshape(shp))
    return o.reshape(bsz * seq, nh * MOBA_DH), w_out.astype(BF16)


def _ret_kernel(lg_ref, q_ref, k_ref, v_ref, gate_ref, gn_ref, o_ref, state_ref, *, c):
    t = pl.program_id(2)

    @pl.when(t == 0)
    def _():
        state_ref[...] = jnp.zeros_like(state_ref)

    lg = lg_ref[0][:, :1]
    q = q_ref[0]
    k = k_ref[0]
    v = v_ref[0]
    ri = lax.broadcasted_iota(jnp.int32, (c, c), 0)
    ci = lax.broadcasted_iota(jnp.int32, (c, c), 1)
    tril = ri >= ci
    dmat = jnp.where(tril, jnp.exp(lg * jnp.where(tril, ri - ci, 0).astype(F32)), 0.0)
    inner = lax.dot_general(q, k, NT_DIMS, preferred_element_type=F32) * dmat
    o = jnp.dot(inner.astype(BF16), v, preferred_element_type=F32)
    pos = lax.broadcasted_iota(jnp.int32, (c, 1), 0).astype(F32)
    st = state_ref[...]
    qd = (q.astype(F32) * jnp.exp(lg * (pos + 1.0))).astype(BF16)
    o = o + jnp.dot(qd, st.astype(BF16), preferred_element_type=F32)
    kd = (k.astype(F32) * jnp.exp(lg * (c - 1.0 - pos))).astype(BF16)
    state_ref[...] = st * jnp.exp(lg * c) + lax.dot_general(kd, v, TN_DIMS, preferred_element_type=F32)

    mu = jnp.mean(o, axis=-1, keepdims=True)
    d = o - mu
    var = jnp.mean(d * d, axis=-1, keepdims=True)
    o = d * lax.rsqrt(var + LN_EPS) * gn_ref[...]
    o_ref[0] = (_silu(gate_ref[0].astype(F32)) * o).astype(o_ref.dtype)


def _ret_core(q, k, v, gate, gn_g, log_gamma):
    bsz, seq, _ = q.shape
    nh = log_gamma.shape[0]
    c = RET_TILE
    lg = jnp.broadcast_to(log_gamma.reshape(nh, 1, 1), (nh, 1, LANES)).astype(F32)
    qk_spec = pl.BlockSpec((1, c, RET_DK), lambda b, h, t: (b, t, h))
    v_spec = pl.BlockSpec((1, c, RET_DV), lambda b, h, t: (b, t, h))
    return pl.pallas_call(
        functools.partial(_ret_kernel, c=c),
        grid=(bsz, nh, seq // c),
        in_specs=[pl.BlockSpec((1, 1, LANES), lambda b, h, t: (h, 0, 0)),
                  qk_spec, qk_spec, v_spec, v_spec,
                  pl.BlockSpec((1, RET_DV), lambda b, h, t: (0, h))],
        out_specs=v_spec,
        out_shape=jax.ShapeDtypeStruct((bsz, seq, nh * RET_DV), BF16),
        scratch_shapes=[pltpu.VMEM((RET_DK, RET_DV), F32)],
        compiler_params=_params("parallel", "parallel", "arbitrary"),
        name="ret_core",
    )(lg, q, k, v, gate, gn_g.astype(F32).reshape(1, nh * RET_DV))


def _ret_layer(h2d, bsz, seq, w_in, gn_g, w_out):
    nh = gn_g.shape[0] // RET_DV
    qw = nh * RET_DK
    vw = nh * RET_DV
    assert seq % RET_TILE == 0
    inv_freq = jnp.power(XPOS_BASE, -jnp.linspace(0.0, 1.0, RET_DK // 2, dtype=F32))
    ang = jnp.arange(seq, dtype=F32)[:, None] * inv_freq[None, :]
    cos, sin = jnp.cos(ang), jnp.sin(ang)
    log_gamma = jnp.log1p(-jnp.exp2(-5.0 - jnp.arange(nh, dtype=F32)))
    wb = w_in.astype(BF16)
    q = _proj_rope_full(h2d, wb[:, :qw], cos, sin, seq, RET_DK, 1.0, 1024, qw)
    k = _proj_rope_full(h2d, wb[:, qw:2 * qw], cos, sin, seq, RET_DK, RET_DK ** -0.5, 1024, qw)
    v = _proj(h2d, wb[:, 2 * qw:2 * qw + vw], BF16, 1024, 1024)
    gate = _proj(h2d, wb[:, 2 * qw + vw:], BF16, 1024, 1024)
    o = _ret_core(q.reshape(bsz, seq, qw), k.reshape(bsz, seq, qw), v.reshape(bsz, seq, vw),
                  gate.reshape(bsz, seq, vw), gn_g, log_gamma)
    return o.reshape(bsz * seq, vw), w_out.astype(BF16)


def kernel(x, a_w_in, a_conv, a_a_log, a_dt_bias, a_norm_g, a_w_out, b_w_qkv, b_w_out,
           c_w_in, c_gn_g, c_w_out, f_w13, f_w2, ln1_g, ln1_b, ln2_g, ln2_b):
    bsz, seq, d = x.shape
    depth = f_w13.shape[0]
    alpha = (2 * depth) ** 0.25
    h = x.reshape(bsz * seq, d)
    for i in range(depth):
        kind, j = i % N_MIXERS, i // N_MIXERS
        if kind == 0:
            o, w_out = _gdn_layer(h, bsz, seq, a_w_in[j], a_conv[j], a_a_log[j], a_dt_bias[j],
                                  a_norm_g[j], a_w_out[j])
        elif kind == 1:
            o, w_out = _moba_layer(h, bsz, seq, b_w_qkv[j], b_w_out[j])
        else:
            o, w_out = _ret_layer(h, bsz, seq, c_w_in[j], c_gn_g[j], c_w_out[j])
        h = _outproj_ln(o, w_out, h, ln1_g[i], ln1_b[i], alpha, 512)
        h = _ffn_ln(h, f_w13[i].astype(BF16), f_w2[i].astype(BF16), ln2_g[i], ln2_b[i], alpha, 512, 256)
    return h.reshape(bsz, seq, d)
```

```python
import functools
import math

import jax
import jax.numpy as jnp
from jax import lax
from jax.experimental import pallas as pl
from jax.experimental.pallas import tpu as pltpu

F32 = jnp.float32
BF16 = jnp.bfloat16

N_MIXERS = 3
LANES = 128
VMEM_LIMIT = 56 * 1024 * 1024

GDN_DK = 128
GDN_CHUNK = 64
GDN_SUPER = 128
GDN_CONV = 4
GDN_WAVE = 4
MOBA_DH = 128
MOBA_BLOCK = 256
MOBA_TOPK = 3
MOBA_GROUP = 4
ONES_ROWS = 16
ROPE_THETA = 500000.0
ROPE_DIMS = MOBA_DH // 4
RET_DK = 256
RET_DV = 512
RET_TILE = 256
XPOS_BASE = 10000.0
LN_EPS = 1e-5
NORM_EPS = 1e-6
NEG_INF = -1e30

NT_DIMS = (((1,), (1,)), ((), ()))
TN_DIMS = (((0,), (0,)), ((), ()))


def _params(*semantics):
    return pltpu.CompilerParams(dimension_semantics=semantics, vmem_limit_bytes=VMEM_LIMIT)


def _silu(x):
    return x * jax.nn.sigmoid(x)


def _proj_kernel(x_ref, w_ref, o_ref):
    x = x_ref[...].astype(BF16)
    o_ref[...] = jnp.dot(x, w_ref[...], preferred_element_type=F32).astype(o_ref.dtype)


def _proj(x2d, w, out_dtype, tm, tn):
    m, k = x2d.shape
    n = w.shape[1]
    return pl.pallas_call(
        _proj_kernel,
        grid=(m // tm, n // tn),
        in_specs=[pl.BlockSpec((tm, k), lambda i, j: (i, 0)),
                  pl.BlockSpec((k, tn), lambda i, j: (0, j))],
        out_specs=pl.BlockSpec((tm, tn), lambda i, j: (i, j)),
        out_shape=jax.ShapeDtypeStruct((m, n), out_dtype),
        compiler_params=_params("parallel", "parallel"),
        name="proj",
    )(x2d, w)


def _proj_rope_partial_kernel(x_ref, w_ref, c_ref, s1_ref, s2_ref, o_ref, *, half, scale):
    x = x_ref[...].astype(BF16)
    acc = jnp.dot(x, w_ref[...], preferred_element_type=F32)
    tn = acc.shape[1]
    reps = tn // LANES
    c = jnp.tile(c_ref[...], (1, reps))
    s1 = jnp.tile(s1_ref[...], (1, reps))
    s2 = jnp.tile(s2_ref[...], (1, reps))
    out = acc * c + pltpu.roll(acc, tn - half, 1) * s1 + pltpu.roll(acc, half, 1) * s2
    o_ref[...] = (out * scale).astype(o_ref.dtype)


def _proj_rope_partial(x2d, w, tabs, seq, scale, tm, tn):
    m, k = x2d.shape
    n = w.shape[1]
    c, s1, s2 = tabs
    half = ROPE_DIMS // 2
    tpos = seq // tm
    tab_spec = pl.BlockSpec((tm, LANES), lambda i, j: (i % tpos, 0))
    return pl.pallas_call(
        functools.partial(_proj_rope_partial_kernel, half=half, scale=scale),
        grid=(m // tm, n // tn),
        in_specs=[pl.BlockSpec((tm, k), lambda i, j: (i, 0)),
                  pl.BlockSpec((k, tn), lambda i, j: (0, j)),
                  tab_spec, tab_spec, tab_spec],
        out_specs=pl.BlockSpec((tm, tn), lambda i, j: (i, j)),
        out_shape=jax.ShapeDtypeStruct((m, n), BF16),
        compiler_params=_params("parallel", "parallel"),
        name="proj_rope_partial",
    )(x2d, w, c, s1, s2)


def _proj_rope_full_kernel(x_ref, w_ref, cos_ref, sin_ref, o_ref, *, dk, scale):
    x = x_ref[...].astype(BF16)
    acc = jnp.dot(x, w_ref[...], preferred_element_type=F32)
    cos = cos_ref[...]
    sin = sin_ref[...]
    half = dk // 2
    for hh in range(acc.shape[1] // dk):
        a = acc[:, hh * dk: hh * dk + half]
        b = acc[:, hh * dk + half: (hh + 1) * dk]
        o_ref[:, hh * dk: hh * dk + half] = ((a * cos - b * sin) * scale).astype(o_ref.dtype)
        o_ref[:, hh * dk + half: (hh + 1) * dk] = ((a * sin + b * cos) * scale).astype(o_ref.dtype)


def _proj_rope_full(x2d, w, cos, sin, seq, dk, scale, tm, tn):
    m, k = x2d.shape
    n = w.shape[1]
    tpos = seq // tm
    tab_spec = pl.BlockSpec((tm, dk // 2), lambda i, j: (i % tpos, 0))
    return pl.pallas_call(
        functools.partial(_proj_rope_full_kernel, dk=dk, scale=scale),
        grid=(m // tm, n // tn),
        in_specs=[pl.BlockSpec((tm, k), lambda i, j: (i, 0)),
                  pl.BlockSpec((k, tn), lambda i, j: (0, j)),
                  tab_spec, tab_spec],
        out_specs=pl.BlockSpec((tm, tn), lambda i, j: (i, j)),
        out_shape=jax.ShapeDtypeStruct((m, n), BF16),
        compiler_params=_params("parallel", "parallel"),
        name="proj_rope_full",
    )(x2d, w, cos, sin)


def _layer_norm(y, g, b):
    mu = jnp.mean(y, axis=-1, keepdims=True)
    d = y - mu
    var = jnp.mean(d * d, axis=-1, keepdims=True)
    return d * lax.rsqrt(var + LN_EPS) * g + b


def _outproj_ln_kernel(o_ref, w_ref, h_ref, g_ref, b_ref, out_ref, *, alpha):
    mix = jnp.dot(o_ref[...], w_ref[...], preferred_element_type=F32)
    y = alpha * h_ref[...] + mix
    out_ref[...] = _layer_norm(y, g_ref[...], b_ref[...])


def _outproj_ln(o2d, w, h2d, g, b, alpha, tm):
    m, kin = o2d.shape
    d = w.shape[1]
    return pl.pallas_call(
        functools.partial(_outproj_ln_kernel, alpha=alpha),
        grid=(m // tm,),
        in_specs=[pl.BlockSpec((tm, kin), lambda i: (i, 0)),
                  pl.BlockSpec((kin, d), lambda i: (0, 0)),
                  pl.BlockSpec((tm, d), lambda i: (i, 0)),
                  pl.BlockSpec((1, d), lambda i: (0, 0)),
                  pl.BlockSpec((1, d), lambda i: (0, 0))],
        out_specs=pl.BlockSpec((tm, d), lambda i: (i, 0)),
        out_shape=jax.ShapeDtypeStruct((m, d), F32),
        compiler_params=_params("parallel"),
        name="outproj_ln",
    )(o2d, w, h2d, g.reshape(1, d), b.reshape(1, d))


def _ffn_ln_kernel(h_ref, w13_ref, w2_ref, g_ref, b_ref, out_ref, *, alpha, dff, fc):
    h = h_ref[...]
    xb = h.astype(BF16)
    acc = alpha * h
    for c in range(dff // fc):
        gate = jnp.dot(xb, w13_ref[:, c * fc:(c + 1) * fc], preferred_element_type=F32)
        up = jnp.dot(xb, w13_ref[:, dff + c * fc: dff + (c + 1) * fc], preferred_element_type=F32)
        act = (_silu(gate) * up).astype(BF16)
        acc = acc + jnp.dot(act, w2_ref[c * fc:(c + 1) * fc, :], preferred_element_type=F32)
    out_ref[...] = _layer_norm(acc, g_ref[...], b_ref[...])


def _ffn_ln(h2d, w13, w2, g, b, alpha, tm, fc):
    m, d = h2d.shape
    dff = w2.shape[0]
    once = pl.Buffered(1)
    return pl.pallas_call(
        functools.partial(_ffn_ln_kernel, alpha=alpha, dff=dff, fc=fc),
        grid=(m // tm,),
        in_specs=[pl.BlockSpec((tm, d), lambda i: (i, 0)),
                  pl.BlockSpec((d, 2 * dff), lambda i: (0, 0), pipeline_mode=once),
                  pl.BlockSpec((dff, d), lambda i: (0, 0), pipeline_mode=once),
                  pl.BlockSpec((1, d), lambda i: (0, 0)),
                  pl.BlockSpec((1, d), lambda i: (0, 0))],
        out_specs=pl.BlockSpec((tm, d), lambda i: (i, 0)),
        out_shape=jax.ShapeDtypeStruct((m, d), F32),
        compiler_params=_params("parallel"),
        name="ffn_ln",
    )(h2d, w13, w2, g.reshape(1, d), b.reshape(1, d))


def _gdn_gates_kernel(g_ref, alog_ref, dtb_ref, o_ref, *, c, nheads):
    gt = g_ref[0]
    tg = gt.shape[0]
    sp_in = gt + dtb_ref[...]
    softplus = jnp.maximum(sp_in, 0.0) + jnp.log1p(jnp.exp(-jnp.abs(sp_in)))
    g = -jnp.exp(alog_ref[...]) * softplus
    ri = lax.broadcasted_iota(jnp.int32, (tg, tg), 0)
    ci = lax.broadcasted_iota(jnp.int32, (tg, tg), 1)
    tri = jnp.where(((ri // c) == (ci // c)) & (ri >= ci), 1.0, 0.0).astype(BF16)
    g1 = g.astype(BF16)
    r1 = g - g1.astype(F32)
    g2 = r1.astype(BF16)
    g3 = (r1 - g2.astype(F32)).astype(BF16)
    gc = (jnp.dot(tri, g1, preferred_element_type=F32) + jnp.dot(tri, g2, preferred_element_type=F32)
          + jnp.dot(tri, g3, preferred_element_type=F32))
    lane = lax.broadcasted_iota(jnp.int32, (tg, LANES), 1)
    o_ref[0] = jnp.where(lane < nheads, jax.nn.sigmoid(gt), gc)


def _gdn_gates(gates, a_log, dt_bias, tg):
    bsz, seq, _ = gates.shape
    nh = a_log.shape[0]
    pad = LANES - 2 * nh
    alog_row = jnp.concatenate([jnp.zeros((nh,), F32), a_log.astype(F32), jnp.zeros((pad,), F32)]).reshape(1, LANES)
    dtb_row = jnp.concatenate([jnp.zeros((nh,), F32), dt_bias.astype(F32), jnp.zeros((pad,), F32)]).reshape(1, LANES)
    tile = pl.BlockSpec((1, tg, LANES), lambda b, t: (b, t, 0))
    row = pl.BlockSpec((1, LANES), lambda b, t: (0, 0))
    return pl.pallas_call(
        functools.partial(_gdn_gates_kernel, c=GDN_CHUNK, nheads=nh),
        grid=(bsz, seq // tg),
        in_specs=[tile, row, row],
        out_specs=tile,
        out_shape=jax.ShapeDtypeStruct(gates.shape, F32),
        compiler_params=_params("parallel", "parallel"),
        name="gdn_gates",
    )(gates, alog_row, dtb_row)


def _gdn_kernel(q_ref, k_ref, v_ref, z_ref, qp_ref, kp_ref, vp_ref, gates_ref,
                wq_ref, wk_ref, wv_ref, ng_ref, o_ref, state_ref,
                *, tt, nheads, hp, wave):
    hg = pl.program_id(1)
    t = pl.program_id(2)
    c = GDN_CHUNK
    sc = GDN_SUPER
    ns = tt // sc
    nc = tt // c
    dk = GDN_DK

    @pl.when(t == 0)
    def _():
        state_ref[...] = jnp.zeros_like(state_ref)

    row16 = lax.broadcasted_iota(jnp.int32, (16, LANES), 0)

    def conv_silu(x, xp, w):
        x = x.astype(F32)
        xp = jnp.where(t > 0, xp.astype(F32), 0.0)
        y = x * w[GDN_CONV - 1:GDN_CONV, :]
        for s in range(1, GDN_CONV):
            xs = pltpu.roll(x, s, 0)
            head = jnp.where(row16 < s, pltpu.roll(xp, s, 0), xs[:16])
            xs = jnp.concatenate([head, xs[16:]], axis=0)
            y = y + xs * w[GDN_CONV - 1 - s:GDN_CONV - s, :]
        return _silu(y)

    def l2n(x):
        return x * lax.rsqrt(jnp.sum(x * x, axis=-1, keepdims=True) + NORM_EPS)

    gb_all = gates_ref[0]
    lane = lax.broadcasted_iota(jnp.int32, (tt, LANES), 1)

    ri = lax.broadcasted_iota(jnp.int32, (sc, sc), 0)
    ci = lax.broadcasted_iota(jnp.int32, (sc, sc), 1)
    same = (ri // c) == (ci // c)
    tril = same & (ri >= ci)
    strict = same & (ri > ci)
    eye = (ri == ci).astype(F32)

    dm, lhs, k_s, rhs, qd_u, kd_u, cd = {}, {}, {}, {}, {}, {}, {}
    o_loc, q_eff, kuw = {}, {}, {}

    def prepare(hh):
        ln = slice(hh * dk, (hh + 1) * dk)
        hd = hg * hp + hh
        qn = l2n(conv_silu(q_ref[0, :, ln], qp_ref[0, :, ln], wq_ref[:, ln])) * (dk ** -0.5)
        kn = l2n(conv_silu(k_ref[0, :, ln], kp_ref[0, :, ln], wk_ref[:, ln]))
        vv = conv_silu(v_ref[0, :, ln], vp_ref[0, :, ln], wv_ref[:, ln])
        beta = jnp.sum(jnp.where(lane == hd, gb_all, 0.0), axis=-1, keepdims=True)
        gcol = jnp.sum(jnp.where(lane == nheads + hd, gb_all, 0.0), axis=-1, keepdims=True)
        glast = jnp.broadcast_to(gcol.reshape(nc, c, 1)[:, c - 1:c, :], (nc, c, 1)).reshape(tt, 1)
        eg = jnp.exp(gcol)
        kb = kn * beta
        kbg = kb * eg
        qd = qn * eg
        kd = kn * jnp.exp(glast - gcol)
        vb = vv * beta
        for sidx in range(ns):
            r = slice(sidx * sc, (sidx + 1) * sc)
            u = (hh, sidx)
            gb = jnp.broadcast_to(gcol[r], (sc, sc))
            dm[u] = jnp.exp(jnp.where(tril, gb - gb.T, NEG_INF))
            k_s[u] = kn[r].astype(BF16)
            lhs[u] = jnp.concatenate([kb[r], qn[r]], axis=0).astype(BF16)
            rhs[u] = jnp.concatenate([vb[r], kbg[r]], axis=1).astype(BF16)
            qd_u[u] = qd[r]
            kd_u[u] = kd[r].astype(BF16)
            for half in range(sc // c):
                row = sidx * sc + half * c
                cd[hh, sidx * (sc // c) + half] = jnp.exp(glast[row:row + 1])

    def wy_transform(units):
        kq = {u: lax.dot_general(lhs[u], k_s[u], NT_DIMS, preferred_element_type=F32) for u in units}
        attn = {u: (kq[u][sc:] * dm[u]).astype(BF16) for u in units}
        low = {u: jnp.where(strict, kq[u][:sc] * dm[u], 0.0) for u in units}
        pw = {u: low[u].astype(BF16) for u in units}
        tinv = {u: eye - low[u] for u in units}
        span = 2
        while span < c:
            pw = {u: jnp.dot(pw[u], pw[u], preferred_element_type=F32).astype(BF16) for u in units}
            tinv = {u: tinv[u] + jnp.dot(tinv[u].astype(BF16), pw[u], preferred_element_type=F32)
                    for u in units}
            span *= 2
        uw_b = {u: jnp.dot(tinv[u].astype(BF16), rhs[u], preferred_element_type=F32).astype(BF16)
                for u in units}
        auw = {u: jnp.dot(attn[u], uw_b[u], preferred_element_type=F32) for u in units}
        for u in units:
            hh, sidx = u
            qe = (qd_u[u] - auw[u][:, dk:]).astype(BF16)
            for half in range(sc // c):
                a = slice(half * c, (half + 1) * c)
                ch = sidx * (sc // c) + half
                o_loc[hh, ch] = auw[u][a, :dk]
                q_eff[hh, ch] = qe[a]
                kuw[hh, ch] = lax.dot_general(kd_u[u][a], uw_b[u][a], TN_DIMS,
                                              preferred_element_type=F32)

    for w0 in range(0, hp, wave):
        heads = range(w0, min(w0 + wave, hp))
        for hh in heads:
            prepare(hh)
        wy_transform([(hh, sidx) for hh in heads for sidx in range(ns)])

    o_parts = [[] for _ in range(hp)]
    for ch in range(nc):
        for hh in range(hp):
            st = state_ref[hh]
            lhs2 = jnp.concatenate([kuw[hh, ch][:, dk:].astype(BF16), q_eff[hh, ch]], axis=0)
            res = jnp.dot(lhs2, st.astype(BF16), preferred_element_type=F32)
            o_parts[hh].append(o_loc[hh, ch] + res[dk:])
            state_ref[hh] = st * cd[hh, ch] + kuw[hh, ch][:, :dk] - res[:dk]

    for hh in range(hp):
        ln = slice(hh * dk, (hh + 1) * dk)
        o = jnp.concatenate(o_parts[hh], axis=0)
        o = o * lax.rsqrt(jnp.mean(o * o, axis=-1, keepdims=True) + NORM_EPS) * ng_ref[...]
        o = o * _silu(z_ref[0, :, ln].astype(F32))
        o_ref[0, :, ln] = o.astype(o_ref.dtype)


def _gdn_core(qkvz, gates, w_conv, norm_g, tt, hp):
    bsz, seq, width = qkvz.shape
    nh = width // (4 * GDN_DK)
    ng = nh // hp
    hw = hp * GDN_DK
    prev = tt // 16

    def cur(off):
        return pl.BlockSpec((1, tt, hw), lambda b, h, t: (b, t, off + h))

    def prv(off):
        return pl.BlockSpec((1, 16, hw), lambda b, h, t: (b, jnp.maximum(t * prev - 1, 0), off + h))

    def cw(off):
        return pl.BlockSpec((GDN_CONV, hw), lambda b, h, t: (0, off + h))

    row = pl.BlockSpec((1, LANES), lambda b, h, t: (0, 0))
    return pl.pallas_call(
        functools.partial(_gdn_kernel, tt=tt, nheads=nh, hp=hp, wave=GDN_WAVE),
        grid=(bsz, ng, seq // tt),
        in_specs=[cur(0), cur(ng), cur(2 * ng), cur(3 * ng),
                  prv(0), prv(ng), prv(2 * ng),
                  pl.BlockSpec((1, tt, LANES), lambda b, h, t: (b, t, 0)),
                  cw(0), cw(ng), cw(2 * ng), row],
        out_specs=pl.BlockSpec((1, tt, hw), lambda b, h, t: (b, t, h)),
        out_shape=jax.ShapeDtypeStruct((bsz, seq, nh * GDN_DK), BF16),
        scratch_shapes=[pltpu.VMEM((hp, GDN_DK, GDN_DK), F32)],
        compiler_params=_params("parallel", "parallel", "arbitrary"),
        name="gdn_core",
    )(qkvz, qkvz, qkvz, qkvz, qkvz, qkvz, qkvz, gates,
      w_conv, w_conv, w_conv, norm_g.astype(F32).reshape(1, LANES))


def _gdn_layer(h2d, bsz, seq, w_in, w_conv, a_log, dt_bias, norm_g, w_out):
    nh = a_log.shape[0]
    main = 4 * nh * GDN_DK
    w_main = w_in[:, :main].astype(BF16)
    w_gate = jnp.pad(w_in[:, main:], ((0, 0), (0, LANES - 2 * nh))).astype(BF16)
    qkvz = _proj(h2d, w_main, BF16, 1024, 1024).reshape(bsz, seq, main)
    gates = _proj(h2d, w_gate, F32, 1024, LANES).reshape(bsz, seq, LANES)
    gates = _gdn_gates(gates, a_log, dt_bias, tg=512)
    o = _gdn_core(qkvz, gates, w_conv.astype(F32), norm_g, tt=256, hp=4)
    return o.reshape(bsz * seq, nh * GDN_DK), w_out.astype(BF16)


def _moba_kernel(q_ref, k_ref, v_ref, o_ref, kmh_ref, kml_ref, vt_ref, *, nb, bs, grp):
    i = pl.program_id(2)
    dh = MOBA_DH
    gw = grp * bs

    @pl.when(i == 0)
    def _():
        kf = k_ref[0].astype(F32).reshape(nb, bs, dh)
        km = jnp.sum(kf, axis=1) * (1.0 / bs)
        hi = km.astype(BF16)
        kmh_ref[...] = hi
        kml_ref[...] = (km - hi.astype(F32)).astype(BF16)

        def tr(j, carry):
            vj = v_ref[0, pl.ds(pl.multiple_of(j * gw, gw), gw), :].astype(F32)
            vt_ref[j, :dh, :] = vj.T.astype(BF16)
            vt_ref[j, dh:, :] = jnp.ones((ONES_ROWS, gw), BF16)
            return carry
        lax.fori_loop(0, nb // grp, tr, 0)

    q = q_ref[0]
    gate = (lax.dot_general(kmh_ref[...], q, NT_DIMS, preferred_element_type=F32) +
            lax.dot_general(kml_ref[...], q, NT_DIMS, preferred_element_type=F32))
    blk = lax.broadcasted_iota(jnp.int32, (nb, bs), 0)
    past = blk < i
    g = jnp.where(past, gate, -jnp.inf)
    sel = jnp.zeros((nb, bs), F32)
    for _ in range(MOBA_TOPK):
        mx = jnp.max(g, axis=0, keepdims=True)
        idx = jnp.min(jnp.where(g == mx, blk, nb), axis=0, keepdims=True)
        hit = blk == idx
        sel = jnp.where(hit, 1.0, sel)
        g = jnp.where(hit, -jnp.inf, g)
    bias = jnp.where(past & (sel > 0.0), 0.0, NEG_INF)
    own = pl.multiple_of(i * bs, bs)
    kpos = lax.broadcasted_iota(jnp.int32, (bs, bs), 0)
    qpos = lax.broadcasted_iota(jnp.int32, (bs, bs), 1)

    def scores(row0):
        return lax.dot_general(k_ref[0, pl.ds(row0, bs), :], q, NT_DIMS, preferred_element_type=F32)

    def attend(ngroups):
        s_own = jnp.where(kpos <= qpos, scores(own), NEG_INF)
        s_next = [scores(u * bs) for u in range(grp)] if ngroups else []
        m = jnp.max(s_own, axis=0, keepdims=True)
        p = jnp.exp2(s_own - m).astype(BF16)
        v_own = jnp.concatenate([v_ref[0, pl.ds(own, bs), :], jnp.ones((bs, ONES_ROWS), BF16)], axis=1)
        acc = lax.dot_general(v_own, p, TN_DIMS, preferred_element_type=F32)
        for gi in range(ngroups):
            s_cur = s_next
            s_next = [scores((gi + 1) * gw + u * bs) for u in range(grp)] if gi + 1 < ngroups else []
            brow = [bias[gi * grp + u:gi * grp + u + 1] for u in range(grp)]
            m_new = m
            for u in range(grp):
                m_new = jnp.maximum(m_new, jnp.max(s_cur[u], axis=0, keepdims=True) + brow[u])
            acc = jnp.exp2(m - m_new) * acc
            for u in range(grp):
                pu = jnp.exp2((s_cur[u] - (m_new - brow[u])).astype(BF16))
                acc = acc + jnp.dot(vt_ref[gi, :, u * bs:(u + 1) * bs], pu, preferred_element_type=F32)
            m = m_new
        o_ref[0] = (acc[:dh] / acc[dh:dh + 1]).T.astype(o_ref.dtype)

    needed = (i + grp - 1) // grp
    for ngroups in range(nb // grp + 1):
        pl.when(needed == ngroups)(functools.partial(attend, ngroups))


def _moba_core(q, k, v):
    bsz, seq, width = q.shape
    nh = width // MOBA_DH
    bs = MOBA_BLOCK
    nb = seq // bs
    grp = math.gcd(nb, MOBA_GROUP)
    full = pl.BlockSpec((1, seq, MOBA_DH), lambda b, h, i: (b, 0, h))
    blk = pl.BlockSpec((1, bs, MOBA_DH), lambda b, h, i: (b, i, h))
    return pl.pallas_call(
        functools.partial(_moba_kernel, nb=nb, bs=bs, grp=grp),
        grid=(bsz, nh, nb),
        in_specs=[blk, full, full],
        out_specs=blk,
        out_shape=jax.ShapeDtypeStruct((bsz, seq, width), BF16),
        scratch_shapes=[pltpu.VMEM((nb, MOBA_DH), BF16), pltpu.VMEM((nb, MOBA_DH), BF16),
                        pltpu.VMEM((nb // grp, MOBA_DH + ONES_ROWS, grp * bs), BF16)],
        compiler_params=_params("parallel", "parallel", "arbitrary"),
        name="moba_core",
    )(q, k, v)


def _moba_rope_tables(seq):
    half = ROPE_DIMS // 2
    inv_freq = jnp.power(ROPE_THETA, -jnp.arange(half, dtype=F32) / half)
    ang = jnp.arange(seq, dtype=F32)[:, None] * inv_freq[None, :]
    cos, sin = jnp.cos(ang), jnp.sin(ang)
    rest = LANES - 2 * half
    c = jnp.concatenate([cos, cos, jnp.ones((seq, rest), F32)], axis=1)
    s1 = jnp.concatenate([-sin, jnp.zeros((seq, LANES - half), F32)], axis=1)
    s2 = jnp.concatenate([jnp.zeros((seq, half), F32), sin, jnp.zeros((seq, rest), F32)], axis=1)
    return c, s1, s2


def _moba_layer(h2d, bsz, seq, w_qkv, w_out):
    width = w_qkv.shape[1] // 3
    nh = width // MOBA_DH
    assert seq % MOBA_BLOCK == 0
    tabs = _moba_rope_tables(seq)
    wq, wk, wv = (w_qkv[:, n * width:(n + 1) * width].astype(BF16) for n in range(3))
    q = _proj_rope_partial(h2d, wq, tabs, seq, MOBA_DH ** -0.5 * math.log2(math.e), 1024, width)
    k = _proj_rope_partial(h2d, wk, tabs, seq, 1.0, 1024, width)
    v = _proj(h2d, wv, BF16, 1024, width)
    shp = (bsz, seq, width)
    o = _moba_core(q.reshape(shp), k.reshape(shp), v.reshape(shp))
    return o.reshape(bsz * seq, nh * MOBA_DH), w_out.astype(BF16)


def _ret_kernel(lg_ref, q_ref, k_ref, v_ref, gate_ref, gn_ref, o_ref, state_ref, dmat_ref, *, c, hp):
    t = pl.program_id(2)
    dk, dv = RET_DK, RET_DV
    heads = range(hp)
    lg = [lg_ref[hh][:, :1] for hh in heads]

    @pl.when(t == 0)
    def _():
        state_ref[...] = jnp.zeros_like(state_ref)
        ri = lax.broadcasted_iota(jnp.int32, (c, c), 0)
        ci = lax.broadcasted_iota(jnp.int32, (c, c), 1)
        tril = ri >= ci
        rel = jnp.where(tril, ri - ci, 0).astype(F32)
        for hh in heads:
            dmat_ref[hh] = jnp.where(tril, jnp.exp(lg[hh] * rel), 0.0)

    pos = lax.broadcasted_iota(jnp.int32, (c, 1), 0).astype(F32)
    q = [q_ref[0, :, hh * dk:(hh + 1) * dk] for hh in heads]
    k = [k_ref[0, :, hh * dk:(hh + 1) * dk] for hh in heads]
    v = [v_ref[0, :, hh * dv:(hh + 1) * dv] for hh in heads]
    st = [state_ref[hh] for hh in heads]
    inner = [lax.dot_general(q[hh], k[hh], NT_DIMS, preferred_element_type=F32) for hh in heads]
    qd = [(q[hh].astype(F32) * jnp.exp(lg[hh] * (pos + 1.0))).astype(BF16) for hh in heads]
    kd = [(k[hh].astype(F32) * jnp.exp(lg[hh] * (c - 1.0 - pos))).astype(BF16) for hh in heads]
    cross = [jnp.dot(qd[hh], st[hh].astype(BF16), preferred_element_type=F32) for hh in heads]
    kv = [lax.dot_general(kd[hh], v[hh], TN_DIMS, preferred_element_type=F32) for hh in heads]
    intra = [jnp.dot((inner[hh] * dmat_ref[hh]).astype(BF16), v[hh], preferred_element_type=F32)
             for hh in heads]
    for hh in heads:
        state_ref[hh] = st[hh] * jnp.exp(lg[hh] * c) + kv[hh]
        o = intra[hh] + cross[hh]
        mu = jnp.mean(o, axis=-1, keepdims=True)
        d = o - mu
        var = jnp.mean(d * d, axis=-1, keepdims=True)
        o = d * lax.rsqrt(var + LN_EPS) * gn_ref[:, hh * dv:(hh + 1) * dv]
        gate = gate_ref[0, :, hh * dv:(hh + 1) * dv].astype(F32)
        o_ref[0, :, hh * dv:(hh + 1) * dv] = (_silu(gate) * o).astype(o_ref.dtype)


def _ret_core(q, k, v, gate, gn_g, log_gamma, hp):
    bsz, seq, _ = q.shape
    nh = log_gamma.shape[0]
    c = RET_TILE
    lg = jnp.broadcast_to(log_gamma.reshape(nh, 1, 1), (nh, 1, LANES)).astype(F32)
    qk_spec = pl.BlockSpec((1, c, hp * RET_DK), lambda b, h, t: (b, t, h))
    v_spec = pl.BlockSpec((1, c, hp * RET_DV), lambda b, h, t: (b, t, h))
    return pl.pallas_call(
        functools.partial(_ret_kernel, c=c, hp=hp),
        grid=(bsz, nh // hp, seq // c),
        in_specs=[pl.BlockSpec((hp, 1, LANES), lambda b, h, t: (h, 0, 0)),
                  qk_spec, qk_spec, v_spec, v_spec,
                  pl.BlockSpec((1, hp * RET_DV), lambda b, h, t: (0, h))],
        out_specs=v_spec,
        out_shape=jax.ShapeDtypeStruct((bsz, seq, nh * RET_DV), BF16),
        scratch_shapes=[pltpu.VMEM((hp, RET_DK, RET_DV), F32), pltpu.VMEM((hp, c, c), F32)],
        compiler_params=_params("parallel", "parallel", "arbitrary"),
        name="ret_core",
    )(lg, q, k, v, gate, gn_g.astype(F32).reshape(1, nh * RET_DV))


def _ret_layer(h2d, bsz, seq, w_in, gn_g, w_out):
    nh = gn_g.shape[0] // RET_DV
    qw = nh * RET_DK
    vw = nh * RET_DV
    assert seq % RET_TILE == 0
    inv_freq = jnp.power(XPOS_BASE, -jnp.linspace(0.0, 1.0, RET_DK // 2, dtype=F32))
    ang = jnp.arange(seq, dtype=F32)[:, None] * inv_freq[None, :]
    cos, sin = jnp.cos(ang), jnp.sin(ang)
    log_gamma = jnp.log1p(-jnp.exp2(-5.0 - jnp.arange(nh, dtype=F32)))
    wb = w_in.astype(BF16)
    q = _proj_rope_full(h2d, wb[:, :qw], cos, sin, seq, RET_DK, 1.0, 1024, qw)
    k = _proj_rope_full(h2d, wb[:, qw:2 * qw], cos, sin, seq, RET_DK, RET_DK ** -0.5, 1024, qw)
    v = _proj(h2d, wb[:, 2 * qw:2 * qw + vw], BF16, 1024, 1024)
    gate = _proj(h2d, wb[:, 2 * qw + vw:], BF16, 1024, 1024)
    o = _ret_core(q.reshape(bsz, seq, qw), k.reshape(bsz, seq, qw), v.reshape(bsz, seq, vw),
                  gate.reshape(bsz, seq, vw), gn_g, log_gamma, hp=2)
    return o.reshape(bsz * seq, vw), w_out.astype(BF16)


def kernel(x, a_w_in, a_conv, a_a_log, a_dt_bias, a_norm_g, a_w_out, b_w_qkv, b_w_out,
           c_w_in, c_gn_g, c_w_out, f_w13, f_w2, ln1_g, ln1_b, ln2_g, ln2_b):
    bsz, seq, d = x.shape
    depth = f_w13.shape[0]
    alpha = (2 * depth) ** 0.25
    h = x.reshape(bsz * seq, d)
    for i in range(depth):
        kind, j = i % N_MIXERS, i // N_MIXERS
        if kind == 0:
            o, w_out = _gdn_layer(h, bsz, seq, a_w_in[j], a_conv[j], a_a_log[j], a_dt_bias[j],
                                  a_norm_g[j], a_w_out[j])
        elif kind == 1:
            o, w_out = _moba_layer(h, bsz, seq, b_w_qkv[j], b_w_out[j])
        else:
            o, w_out = _ret_layer(h, bsz, seq, c_w_in[j], c_gn_g[j], c_w_out[j])
        h = _outproj_ln(o, w_out, h, ln1_g[i], ln1_b[i], alpha, 512)
        h = _ffn_ln(h, f_w13[i].astype(BF16), f_w2[i].astype(BF16), ln2_g[i], ln2_b[i], alpha, 512, 256)
    return h.reshape(bsz, seq, d)
```

```python
import functools
import math

import jax
import jax.numpy as jnp
from jax import lax
from jax.experimental import pallas as pl
from jax.experimental.pallas import tpu as pltpu

F32 = jnp.float32
BF16 = jnp.bfloat16

N_MIXERS = 3
LANES = 128
VMEM_LIMIT = 56 * 1024 * 1024

GDN_DK = 128
GDN_CHUNK = 64
GDN_SUPER = 128
GDN_CONV = 4
GDN_WAVE = 4
MOBA_DH = 128
MOBA_BLOCK = 256
MOBA_TOPK = 3
MOBA_GROUP = 4
ONES_ROWS = 16
ROPE_THETA = 500000.0
ROPE_DIMS = MOBA_DH // 4
RET_DK = 256
RET_DV = 512
RET_TILE = 256
XPOS_BASE = 10000.0
LN_EPS = 1e-5
NORM_EPS = 1e-6
NEG_INF = -1e30

NT_DIMS = (((1,), (1,)), ((), ()))
TN_DIMS = (((0,), (0,)), ((), ()))


def _params(*semantics):
    return pltpu.CompilerParams(dimension_semantics=semantics, vmem_limit_bytes=VMEM_LIMIT)


def _silu(x):
    return x * jax.nn.sigmoid(x)


def _proj_kernel(x_ref, w_ref, o_ref):
    x = x_ref[...].astype(BF16)
    o_ref[...] = jnp.dot(x, w_ref[...], preferred_element_type=F32).astype(o_ref.dtype)


def _proj(x2d, w, out_dtype, tm, tn):
    m, k = x2d.shape
    n = w.shape[1]
    return pl.pallas_call(
        _proj_kernel,
        grid=(m // tm, n // tn),
        in_specs=[pl.BlockSpec((tm, k), lambda i, j: (i, 0)),
                  pl.BlockSpec((k, tn), lambda i, j: (0, j))],
        out_specs=pl.BlockSpec((tm, tn), lambda i, j: (i, j)),
        out_shape=jax.ShapeDtypeStruct((m, n), out_dtype),
        compiler_params=_params("parallel", "parallel"),
        name="proj",
    )(x2d, w)


def _proj_rope_partial_kernel(x_ref, w_ref, c_ref, s1_ref, s2_ref, o_ref, *, half, scale):
    x = x_ref[...].astype(BF16)
    acc = jnp.dot(x, w_ref[...], preferred_element_type=F32)
    tn = acc.shape[1]
    reps = tn // LANES
    c = jnp.tile(c_ref[...], (1, reps))
    s1 = jnp.tile(s1_ref[...], (1, reps))
    s2 = jnp.tile(s2_ref[...], (1, reps))
    out = acc * c + pltpu.roll(acc, tn - half, 1) * s1 + pltpu.roll(acc, half, 1) * s2
    o_ref[...] = (out * scale).astype(o_ref.dtype)


def _proj_rope_partial(x2d, w, tabs, seq, scale, tm, tn):
    m, k = x2d.shape
    n = w.shape[1]
    c, s1, s2 = tabs
    half = ROPE_DIMS // 2
    tpos = seq // tm
    tab_spec = pl.BlockSpec((tm, LANES), lambda i, j: (i % tpos, 0))
    return pl.pallas_call(
        functools.partial(_proj_rope_partial_kernel, half=half, scale=scale),
        grid=(m // tm, n // tn),
        in_specs=[pl.BlockSpec((tm, k), lambda i, j: (i, 0)),
                  pl.BlockSpec((k, tn), lambda i, j: (0, j)),
                  tab_spec, tab_spec, tab_spec],
        out_specs=pl.BlockSpec((tm, tn), lambda i, j: (i, j)),
        out_shape=jax.ShapeDtypeStruct((m, n), BF16),
        compiler_params=_params("parallel", "parallel"),
        name="proj_rope_partial",
    )(x2d, w, c, s1, s2)


def _proj_rope_full_kernel(x_ref, w_ref, cos_ref, sin_ref, o_ref, *, dk, scale):
    x = x_ref[...].astype(BF16)
    acc = jnp.dot(x, w_ref[...], preferred_element_type=F32)
    cos = cos_ref[...]
    sin = sin_ref[...]
    half = dk // 2
    for hh in range(acc.shape[1] // dk):
        a = acc[:, hh * dk: hh * dk + half]
        b = acc[:, hh * dk + half: (hh + 1) * dk]
        o_ref[:, hh * dk: hh * dk + half] = ((a * cos - b * sin) * scale).astype(o_ref.dtype)
        o_ref[:, hh * dk + half: (hh + 1) * dk] = ((a * sin + b * cos) * scale).astype(o_ref.dtype)


def _proj_rope_full(x2d, w, cos, sin, seq, dk, scale, tm, tn):
    m, k = x2d.shape
    n = w.shape[1]
    tpos = seq // tm
    tab_spec = pl.BlockSpec((tm, dk // 2), lambda i, j: (i % tpos, 0))
    return pl.pallas_call(
        functools.partial(_proj_rope_full_kernel, dk=dk, scale=scale),
        grid=(m // tm, n // tn),
        in_specs=[pl.BlockSpec((tm, k), lambda i, j: (i, 0)),
                  pl.BlockSpec((k, tn), lambda i, j: (0, j)),
                  tab_spec, tab_spec],
        out_specs=pl.BlockSpec((tm, tn), lambda i, j: (i, j)),
        out_shape=jax.ShapeDtypeStruct((m, n), BF16),
        compiler_params=_params("parallel", "parallel"),
        name="proj_rope_full",
    )(x2d, w, cos, sin)


def _layer_norm(y, g, b):
    mu = jnp.mean(y, axis=-1, keepdims=True)
    d = y - mu
    var = jnp.mean(d * d, axis=-1, keepdims=True)
    return d * lax.rsqrt(var + LN_EPS) * g + b


def _block_tail_kernel(o_ref, wo_ref, h_ref, g1_ref, b1_ref, w13_ref, w2_ref, g2_ref, b2_ref,
                       out_ref, outb_ref, *, alpha, dff, fc):
    mix = jnp.dot(o_ref[...], wo_ref[...], preferred_element_type=F32)
    h1 = _layer_norm(alpha * h_ref[...] + mix, g1_ref[...], b1_ref[...])
    xb = h1.astype(BF16)
    acc = alpha * h1
    for c in range(dff // fc):
        gate = jnp.dot(xb, w13_ref[:, c * fc:(c + 1) * fc], preferred_element_type=F32)
        up = jnp.dot(xb, w13_ref[:, dff + c * fc: dff + (c + 1) * fc], preferred_element_type=F32)
        act = (_silu(gate) * up).astype(BF16)
        acc = acc + jnp.dot(act, w2_ref[c * fc:(c + 1) * fc, :], preferred_element_type=F32)
    out = _layer_norm(acc, g2_ref[...], b2_ref[...])
    out_ref[...] = out
    outb_ref[...] = out.astype(BF16)


def _block_tail(o2d, wo, h2d, g1, b1, w13, w2, g2, b2, alpha, tm, fc):
    m, kin = o2d.shape
    d = wo.shape[1]
    dff = w2.shape[0]
    once = pl.Buffered(1)
    vec = pl.BlockSpec((1, d), lambda i: (0, 0))
    rows = pl.BlockSpec((tm, d), lambda i: (i, 0))
    return pl.pallas_call(
        functools.partial(_block_tail_kernel, alpha=alpha, dff=dff, fc=fc),
        grid=(m // tm,),
        in_specs=[pl.BlockSpec((tm, kin), lambda i: (i, 0)),
                  pl.BlockSpec((kin, d), lambda i: (0, 0), pipeline_mode=once),
                  rows, vec, vec,
                  pl.BlockSpec((d, 2 * dff), lambda i: (0, 0), pipeline_mode=once),
                  pl.BlockSpec((dff, d), lambda i: (0, 0), pipeline_mode=once),
                  vec, vec],
        out_specs=[rows, rows],
        out_shape=[jax.ShapeDtypeStruct((m, d), F32), jax.ShapeDtypeStruct((m, d), BF16)],
        compiler_params=_params("parallel"),
        name="block_tail",
    )(o2d, wo, h2d, g1.reshape(1, d), b1.reshape(1, d), w13, w2, g2.reshape(1, d), b2.reshape(1, d))


def _gdn_gates_kernel(g_ref, alog_ref, dtb_ref, o_ref, *, c, nheads):
    gt = g_ref[0]
    tg = gt.shape[0]
    sp_in = gt + dtb_ref[...]
    softplus = jnp.maximum(sp_in, 0.0) + jnp.log1p(jnp.exp(-jnp.abs(sp_in)))
    g = -jnp.exp(alog_ref[...]) * softplus
    ri = lax.broadcasted_iota(jnp.int32, (tg, tg), 0)
    ci = lax.broadcasted_iota(jnp.int32, (tg, tg), 1)
    tri = jnp.where(((ri // c) == (ci // c)) & (ri >= ci), 1.0, 0.0).astype(BF16)
    g1 = g.astype(BF16)
    r1 = g - g1.astype(F32)
    g2 = r1.astype(BF16)
    g3 = (r1 - g2.astype(F32)).astype(BF16)
    gc = (jnp.dot(tri, g1, preferred_element_type=F32) + jnp.dot(tri, g2, preferred_element_type=F32)
          + jnp.dot(tri, g3, preferred_element_type=F32))
    lane = lax.broadcasted_iota(jnp.int32, (tg, LANES), 1)
    o_ref[0] = jnp.where(lane < nheads, jax.nn.sigmoid(gt), gc)


def _gdn_gates(gates, a_log, dt_bias, tg):
    bsz, seq, _ = gates.shape
    nh = a_log.shape[0]
    pad = LANES - 2 * nh
    alog_row = jnp.concatenate([jnp.zeros((nh,), F32), a_log.astype(F32), jnp.zeros((pad,), F32)]).reshape(1, LANES)
    dtb_row = jnp.concatenate([jnp.zeros((nh,), F32), dt_bias.astype(F32), jnp.zeros((pad,), F32)]).reshape(1, LANES)
    tile = pl.BlockSpec((1, tg, LANES), lambda b, t: (b, t, 0))
    row = pl.BlockSpec((1, LANES), lambda b, t: (0, 0))
    return pl.pallas_call(
        functools.partial(_gdn_gates_kernel, c=GDN_CHUNK, nheads=nh),
        grid=(bsz, seq // tg),
        in_specs=[tile, row, row],
        out_specs=tile,
        out_shape=jax.ShapeDtypeStruct(gates.shape, F32),
        compiler_params=_params("parallel", "parallel"),
        name="gdn_gates",
    )(gates, alog_row, dtb_row)


def _gdn_kernel(q_ref, k_ref, v_ref, z_ref, qp_ref, kp_ref, vp_ref, gates_ref,
                wq_ref, wk_ref, wv_ref, ng_ref, o_ref, state_ref,
                *, tt, nheads, hp, wave):
    hg = pl.program_id(1)
    t = pl.program_id(2)
    c = GDN_CHUNK
    sc = GDN_SUPER
    ns = tt // sc
    nc = tt // c
    dk = GDN_DK

    @pl.when(t == 0)
    def _():
        state_ref[...] = jnp.zeros_like(state_ref)

    row16 = lax.broadcasted_iota(jnp.int32, (16, LANES), 0)

    def conv_silu(x, xp, w):
        x = x.astype(F32)
        xp = jnp.where(t > 0, xp.astype(F32), 0.0)
        y = x * w[GDN_CONV - 1:GDN_CONV, :]
        for s in range(1, GDN_CONV):
            xs = pltpu.roll(x, s, 0)
            head = jnp.where(row16 < s, pltpu.roll(xp, s, 0), xs[:16])
            xs = jnp.concatenate([head, xs[16:]], axis=0)
            y = y + xs * w[GDN_CONV - 1 - s:GDN_CONV - s, :]
        return _silu(y)

    def l2n(x):
        return x * lax.rsqrt(jnp.sum(x * x, axis=-1, keepdims=True) + NORM_EPS)

    gb_all = gates_ref[0]
    lane = lax.broadcasted_iota(jnp.int32, (tt, LANES), 1)

    ri = lax.broadcasted_iota(jnp.int32, (sc, sc), 0)
    ci = lax.broadcasted_iota(jnp.int32, (sc, sc), 1)
    same = (ri // c) == (ci // c)
    tril = same & (ri >= ci)
    strict = same & (ri > ci)
    eye = (ri == ci).astype(F32)

    dm, lhs, k_s, rhs, qd_u, kd_u, cd = {}, {}, {}, {}, {}, {}, {}
    o_loc, q_eff, kuw = {}, {}, {}

    def prepare(hh):
        ln = slice(hh * dk, (hh + 1) * dk)
        hd = hg * hp + hh
        qn = l2n(conv_silu(q_ref[0, :, ln], qp_ref[0, :, ln], wq_ref[:, ln])) * (dk ** -0.5)
        kn = l2n(conv_silu(k_ref[0, :, ln], kp_ref[0, :, ln], wk_ref[:, ln]))
        vv = conv_silu(v_ref[0, :, ln], vp_ref[0, :, ln], wv_ref[:, ln])
        beta = jnp.sum(jnp.where(lane == hd, gb_all, 0.0), axis=-1, keepdims=True)
        gcol = jnp.sum(jnp.where(lane == nheads + hd, gb_all, 0.0), axis=-1, keepdims=True)
        glast = jnp.broadcast_to(gcol.reshape(nc, c, 1)[:, c - 1:c, :], (nc, c, 1)).reshape(tt, 1)
        eg = jnp.exp(gcol)
        kb = kn * beta
        kbg = kb * eg
        qd = qn * eg
        kd = kn * jnp.exp(glast - gcol)
        vb = vv * beta
        for sidx in range(ns):
            r = slice(sidx * sc, (sidx + 1) * sc)
            u = (hh, sidx)
            gb = jnp.broadcast_to(gcol[r], (sc, sc))
            dm[u] = jnp.exp(jnp.where(tril, gb - gb.T, NEG_INF))
            k_s[u] = kn[r].astype(BF16)
            lhs[u] = jnp.concatenate([kb[r], qn[r]], axis=0).astype(BF16)
            rhs[u] = jnp.concatenate([vb[r], kbg[r]], axis=1).astype(BF16)
            qd_u[u] = qd[r]
            kd_u[u] = kd[r].astype(BF16)
            for half in range(sc // c):
                row = sidx * sc + half * c
                cd[hh, sidx * (sc // c) + half] = jnp.exp(glast[row:row + 1])

    def wy_transform(units):
        kq = {u: lax.dot_general(lhs[u], k_s[u], NT_DIMS, preferred_element_type=F32) for u in units}
        attn = {u: (kq[u][sc:] * dm[u]).astype(BF16) for u in units}
        low = {u: jnp.where(strict, kq[u][:sc] * dm[u], 0.0) for u in units}
        pw = {u: low[u].astype(BF16) for u in units}
        tinv = {u: eye - low[u] for u in units}
        span = 2
        while span < c:
            pw = {u: jnp.dot(pw[u], pw[u], preferred_element_type=F32).astype(BF16) for u in units}
            tinv = {u: tinv[u] + jnp.dot(tinv[u].astype(BF16), pw[u], preferred_element_type=F32)
                    for u in units}
            span *= 2
        uw_b = {u: jnp.dot(tinv[u].astype(BF16), rhs[u], preferred_element_type=F32).astype(BF16)
                for u in units}
        auw = {u: jnp.dot(attn[u], uw_b[u], preferred_element_type=F32) for u in units}
        for u in units:
            hh, sidx = u
            qe = (qd_u[u] - auw[u][:, dk:]).astype(BF16)
            for half in range(sc // c):
                a = slice(half * c, (half + 1) * c)
                ch = sidx * (sc // c) + half
                o_loc[hh, ch] = auw[u][a, :dk]
                q_eff[hh, ch] = qe[a]
                kuw[hh, ch] = lax.dot_general(kd_u[u][a], uw_b[u][a], TN_DIMS,
                                              preferred_element_type=F32)

    for w0 in range(0, hp, wave):
        heads = range(w0, min(w0 + wave, hp))
        for hh in heads:
            prepare(hh)
        wy_transform([(hh, sidx) for hh in heads for sidx in range(ns)])

    o_parts = [[] for _ in range(hp)]
    for ch in range(nc):
        for hh in range(hp):
            st = state_ref[hh]
            lhs2 = jnp.concatenate([kuw[hh, ch][:, dk:].astype(BF16), q_eff[hh, ch]], axis=0)
            res = jnp.dot(lhs2, st.astype(BF16), preferred_element_type=F32)
            o_parts[hh].append(o_loc[hh, ch] + res[dk:])
            state_ref[hh] = st * cd[hh, ch] + kuw[hh, ch][:, :dk] - res[:dk]

    for hh in range(hp):
        ln = slice(hh * dk, (hh + 1) * dk)
        o = jnp.concatenate(o_parts[hh], axis=0)
        o = o * lax.rsqrt(jnp.mean(o * o, axis=-1, keepdims=True) + NORM_EPS) * ng_ref[...]
        o = o * _silu(z_ref[0, :, ln].astype(F32))
        o_ref[0, :, ln] = o.astype(o_ref.dtype)


def _gdn_core(qkvz, gates, w_conv, norm_g, tt, hp):
    bsz, seq, width = qkvz.shape
    nh = width // (4 * GDN_DK)
    ng = nh // hp
    hw = hp * GDN_DK
    prev = tt // 16

    def cur(off):
        return pl.BlockSpec((1, tt, hw), lambda b, h, t: (b, t, off + h))

    def prv(off):
        return pl.BlockSpec((1, 16, hw), lambda b, h, t: (b, jnp.maximum(t * prev - 1, 0), off + h))

    def cw(off):
        return pl.BlockSpec((GDN_CONV, hw), lambda b, h, t: (0, off + h))

    row = pl.BlockSpec((1, LANES), lambda b, h, t: (0, 0))
    return pl.pallas_call(
        functools.partial(_gdn_kernel, tt=tt, nheads=nh, hp=hp, wave=GDN_WAVE),
        grid=(bsz, ng, seq // tt),
        in_specs=[cur(0), cur(ng), cur(2 * ng), cur(3 * ng),
                  prv(0), prv(ng), prv(2 * ng),
                  pl.BlockSpec((1, tt, LANES), lambda b, h, t: (b, t, 0)),
                  cw(0), cw(ng), cw(2 * ng), row],
        out_specs=pl.BlockSpec((1, tt, hw), lambda b, h, t: (b, t, h)),
        out_shape=jax.ShapeDtypeStruct((bsz, seq, nh * GDN_DK), BF16),
        scratch_shapes=[pltpu.VMEM((hp, GDN_DK, GDN_DK), F32)],
        compiler_params=_params("parallel", "parallel", "arbitrary"),
        name="gdn_core",
    )(qkvz, qkvz, qkvz, qkvz, qkvz, qkvz, qkvz, gates,
      w_conv, w_conv, w_conv, norm_g.astype(F32).reshape(1, LANES))


def _gdn_layer(h2d, bsz, seq, w_in, w_conv, a_log, dt_bias, norm_g, w_out):
    nh = a_log.shape[0]
    main = 4 * nh * GDN_DK
    w_main = w_in[:, :main].astype(BF16)
    w_gate = jnp.pad(w_in[:, main:], ((0, 0), (0, LANES - 2 * nh))).astype(BF16)
    qkvz = _proj(h2d, w_main, BF16, 1024, 1024).reshape(bsz, seq, main)
    gates = _proj(h2d, w_gate, F32, 1024, LANES).reshape(bsz, seq, LANES)
    gates = _gdn_gates(gates, a_log, dt_bias, tg=512)
    o = _gdn_core(qkvz, gates, w_conv.astype(F32), norm_g, tt=256, hp=4)
    return o.reshape(bsz * seq, nh * GDN_DK), w_out.astype(BF16)


def _moba_kernel(q_ref, k_ref, v_ref, o_ref, kmh_ref, kml_ref, vt_ref, *, nb, bs, grp, hp):
    i = pl.program_id(2)
    dh = MOBA_DH
    gw = grp * bs
    heads = range(hp)

    @pl.when(i == 0)
    def _():
        for hh in heads:
            ln = slice(hh * dh, (hh + 1) * dh)
            kf = k_ref[0, :, ln].astype(F32).reshape(nb, bs, dh)
            km = jnp.sum(kf, axis=1) * (1.0 / bs)
            hi = km.astype(BF16)
            kmh_ref[hh] = hi
            kml_ref[hh] = (km - hi.astype(F32)).astype(BF16)

            def tr(j, carry):
                vj = v_ref[0, pl.ds(pl.multiple_of(j * gw, gw), gw), ln].astype(F32)
                vjt = vj.T.astype(BF16)
                for u in range(grp):
                    vt_ref[hh, j * grp + u, :dh, :] = vjt[:, u * bs:(u + 1) * bs]
                    vt_ref[hh, j * grp + u, dh:, :] = jnp.ones((ONES_ROWS, bs), BF16)
                return carry
            lax.fori_loop(0, nb // grp, tr, 0)

    q = [q_ref[0, :, hh * dh:(hh + 1) * dh] for hh in heads]
    own = pl.multiple_of(i * bs, bs)
    kpos = lax.broadcasted_iota(jnp.int32, (bs, bs), 0)
    qpos = lax.broadcasted_iota(jnp.int32, (bs, bs), 1)

    def select_bias(hh):
        gate = (lax.dot_general(kmh_ref[hh], q[hh], NT_DIMS, preferred_element_type=F32) +
                lax.dot_general(kml_ref[hh], q[hh], NT_DIMS, preferred_element_type=F32))
        blk = lax.broadcasted_iota(jnp.int32, (nb, bs), 0)
        past = blk < i
        g = jnp.where(past, gate, -jnp.inf)
        sel = jnp.zeros((nb, bs), F32)
        for _ in range(MOBA_TOPK):
            mx = jnp.max(g, axis=0, keepdims=True)
            idx = jnp.min(jnp.where(g == mx, blk, nb), axis=0, keepdims=True)
            hit = blk == idx
            sel = jnp.where(hit, 1.0, sel)
            g = jnp.where(hit, -jnp.inf, g)
        return jnp.where(past & (sel > 0.0), 0.0, NEG_INF)

    def scores(hh, row0):
        return lax.dot_general(k_ref[0, pl.ds(row0, bs), hh * dh:(hh + 1) * dh], q[hh], NT_DIMS,
                               preferred_element_type=F32)

    def attend(ngroups):
        s_own = [jnp.where(kpos <= qpos, scores(hh, own), NEG_INF) for hh in heads]
        s_next = [[scores(hh, u * bs) for u in range(grp)] if ngroups else [] for hh in heads]
        bias = [select_bias(hh) if ngroups else None for hh in heads]
        m = [jnp.max(s_own[hh], axis=0, keepdims=True) for hh in heads]
        p = [jnp.exp2(s_own[hh] - m[hh]).astype(BF16) for hh in heads]
        acc = [jnp.dot(vt_ref[hh, i], p[hh], preferred_element_type=F32) for hh in heads]
        for gi in range(ngroups):
            s_cur = s_next
            s_next = [[scores(hh, (gi + 1) * gw + u * bs) for u in range(grp)] if gi + 1 < ngroups else []
                      for hh in heads]
            brow = [[bias[hh][gi * grp + u:gi * grp + u + 1] for u in range(grp)] for hh in heads]
            m_new = list(m)
            for u in range(grp):
                for hh in heads:
                    m_new[hh] = jnp.maximum(m_new[hh],
                                            jnp.max(s_cur[hh][u], axis=0, keepdims=True) + brow[hh][u])
            acc = [jnp.exp2(m[hh] - m_new[hh]) * acc[hh] for hh in heads]
            for u in range(grp):
                for hh in heads:
                    pu = jnp.exp2((s_cur[hh][u] - (m_new[hh] - brow[hh][u])).astype(BF16))
                    acc[hh] = acc[hh] + jnp.dot(vt_ref[hh, gi * grp + u], pu, preferred_element_type=F32)
            m = m_new
        for hh in heads:
            o_ref[0, :, hh * dh:(hh + 1) * dh] = (acc[hh][:dh] / acc[hh][dh:dh + 1]).T.astype(o_ref.dtype)

    needed = (i + grp - 1) // grp
    for ngroups in range(nb // grp + 1):
        pl.when(needed == ngroups)(functools.partial(attend, ngroups))


def _moba_core(q, k, v, hp):
    bsz, seq, width = q.shape
    nh = width // MOBA_DH
    bs = MOBA_BLOCK
    nb = seq // bs
    grp = math.gcd(nb, MOBA_GROUP)
    hw = hp * MOBA_DH
    full = pl.BlockSpec((1, seq, hw), lambda b, h, i: (b, 0, h))
    blk = pl.BlockSpec((1, bs, hw), lambda b, h, i: (b, i, h))
    return pl.pallas_call(
        functools.partial(_moba_kernel, nb=nb, bs=bs, grp=grp, hp=hp),
        grid=(bsz, nh // hp, nb),
        in_specs=[blk, full, full],
        out_specs=blk,
        out_shape=jax.ShapeDtypeStruct((bsz, seq, width), BF16),
        scratch_shapes=[pltpu.VMEM((hp, nb, MOBA_DH), BF16), pltpu.VMEM((hp, nb, MOBA_DH), BF16),
                        pltpu.VMEM((hp, nb, MOBA_DH + ONES_ROWS, bs), BF16)],
        compiler_params=_params("parallel", "parallel", "arbitrary"),
        name="moba_core",
    )(q, k, v)


def _moba_rope_tables(seq):
    half = ROPE_DIMS // 2
    inv_freq = jnp.power(ROPE_THETA, -jnp.arange(half, dtype=F32) / half)
    ang = jnp.arange(seq, dtype=F32)[:, None] * inv_freq[None, :]
    cos, sin = jnp.cos(ang), jnp.sin(ang)
    rest = LANES - 2 * half
    c = jnp.concatenate([cos, cos, jnp.ones((seq, rest), F32)], axis=1)
    s1 = jnp.concatenate([-sin, jnp.zeros((seq, LANES - half), F32)], axis=1)
    s2 = jnp.concatenate([jnp.zeros((seq, half), F32), sin, jnp.zeros((seq, rest), F32)], axis=1)
    return c, s1, s2


def _moba_layer(h2d, bsz, seq, w_qkv, w_out):
    width = w_qkv.shape[1] // 3
    nh = width // MOBA_DH
    assert seq % MOBA_BLOCK == 0
    tabs = _moba_rope_tables(seq)
    wq, wk, wv = (w_qkv[:, n * width:(n + 1) * width].astype(BF16) for n in range(3))
    q = _proj_rope_partial(h2d, wq, tabs, seq, MOBA_DH ** -0.5 * math.log2(math.e), 1024, width)
    k = _proj_rope_partial(h2d, wk, tabs, seq, 1.0, 1024, width)
    v = _proj(h2d, wv, BF16, 1024, width)
    shp = (bsz, seq, width)
    o = _moba_core(q.reshape(shp), k.reshape(shp), v.reshape(shp), hp=2)
    return o.reshape(bsz * seq, nh * MOBA_DH), w_out.astype(BF16)


def _ret_kernel(lg_ref, q_ref, k_ref, v_ref, gate_ref, gn_ref, o_ref, state_ref, dmat_ref, *, c, hp):
    t = pl.program_id(2)
    dk, dv = RET_DK, RET_DV
    heads = range(hp)
    lg = [lg_ref[hh][:, :1] for hh in heads]

    @pl.when(t == 0)
    def _():
        state_ref[...] = jnp.zeros_like(state_ref)
        ri = lax.broadcasted_iota(jnp.int32, (c, c), 0)
        ci = lax.broadcasted_iota(jnp.int32, (c, c), 1)
        tril = ri >= ci
        rel = jnp.where(tril, ri - ci, 0).astype(F32)
        for hh in heads:
            dmat_ref[hh] = jnp.where(tril, jnp.exp(lg[hh] * rel), 0.0)

    pos = lax.broadcasted_iota(jnp.int32, (c, 1), 0).astype(F32)
    q = [q_ref[0, :, hh * dk:(hh + 1) * dk] for hh in heads]
    k = [k_ref[0, :, hh * dk:(hh + 1) * dk] for hh in heads]
    v = [v_ref[0, :, hh * dv:(hh + 1) * dv] for hh in heads]
    st = [state_ref[hh] for hh in heads]
    inner = [lax.dot_general(q[hh], k[hh], NT_DIMS, preferred_element_type=F32) for hh in heads]
    qd = [(q[hh].astype(F32) * jnp.exp(lg[hh] * (pos + 1.0))).astype(BF16) for hh in heads]
    kd = [(k[hh].astype(F32) * jnp.exp(lg[hh] * (c - 1.0 - pos))).astype(BF16) for hh in heads]
    cross = [jnp.dot(qd[hh], st[hh].astype(BF16), preferred_element_type=F32) for hh in heads]
    kv = [lax.dot_general(kd[hh], v[hh], TN_DIMS, preferred_element_type=F32) for hh in heads]
    intra = [jnp.dot((inner[hh] * dmat_ref[hh]).astype(BF16), v[hh], preferred_element_type=F32)
             for hh in heads]
    for hh in heads:
        state_ref[hh] = st[hh] * jnp.exp(lg[hh] * c) + kv[hh]
        o = intra[hh] + cross[hh]
        mu = jnp.mean(o, axis=-1, keepdims=True)
        d = o - mu
        var = jnp.mean(d * d, axis=-1, keepdims=True)
        o = d * lax.rsqrt(var + LN_EPS) * gn_ref[:, hh * dv:(hh + 1) * dv]
        gate = gate_ref[0, :, hh * dv:(hh + 1) * dv].astype(F32)
        o_ref[0, :, hh * dv:(hh + 1) * dv] = (_silu(gate) * o).astype(o_ref.dtype)


def _ret_core(q, k, v, gate, gn_g, log_gamma, hp):
    bsz, seq, _ = q.shape
    nh = log_gamma.shape[0]
    c = RET_TILE
    lg = jnp.broadcast_to(log_gamma.reshape(nh, 1, 1), (nh, 1, LANES)).astype(F32)
    qk_spec = pl.BlockSpec((1, c, hp * RET_DK), lambda b, h, t: (b, t, h))
    v_spec = pl.BlockSpec((1, c, hp * RET_DV), lambda b, h, t: (b, t, h))
    return pl.pallas_call(
        functools.partial(_ret_kernel, c=c, hp=hp),
        grid=(bsz, nh // hp, seq // c),
        in_specs=[pl.BlockSpec((hp, 1, LANES), lambda b, h, t: (h, 0, 0)),
                  qk_spec, qk_spec, v_spec, v_spec,
                  pl.BlockSpec((1, hp * RET_DV), lambda b, h, t: (0, h))],
        out_specs=v_spec,
        out_shape=jax.ShapeDtypeStruct((bsz, seq, nh * RET_DV), BF16),
        scratch_shapes=[pltpu.VMEM((hp, RET_DK, RET_DV), F32), pltpu.VMEM((hp, c, c), F32)],
        compiler_params=_params("parallel", "parallel", "arbitrary"),
        name="ret_core",
    )(lg, q, k, v, gate, gn_g.astype(F32).reshape(1, nh * RET_DV))


def _ret_layer(h2d, bsz, seq, w_in, gn_g, w_out):
    nh = gn_g.shape[0] // RET_DV
    qw = nh * RET_DK
    vw = nh * RET_DV
    assert seq % RET_TILE == 0
    inv_freq = jnp.power(XPOS_BASE, -jnp.linspace(0.0, 1.0, RET_DK // 2, dtype=F32))
    ang = jnp.arange(seq, dtype=F32)[:, None] * inv_freq[None, :]
    cos, sin = jnp.cos(ang), jnp.sin(ang)
    log_gamma = jnp.log1p(-jnp.exp2(-5.0 - jnp.arange(nh, dtype=F32)))
    wb = w_in.astype(BF16)
    q = _proj_rope_full(h2d, wb[:, :qw], cos, sin, seq, RET_DK, 1.0, 1024, qw)
    k = _proj_rope_full(h2d, wb[:, qw:2 * qw], cos, sin, seq, RET_DK, RET_DK ** -0.5, 1024, qw)
    v = _proj(h2d, wb[:, 2 * qw:2 * qw + vw], BF16, 1024, 1024)
    gate = _proj(h2d, wb[:, 2 * qw + vw:], BF16, 1024, 1024)
    o = _ret_core(q.reshape(bsz, seq, qw), k.reshape(bsz, seq, qw), v.reshape(bsz, seq, vw),
                  gate.reshape(bsz, seq, vw), gn_g, log_gamma, hp=2)
    return o.reshape(bsz * seq, vw), w_out.astype(BF16)


def kernel(x, a_w_in, a_conv, a_a_log, a_dt_bias, a_norm_g, a_w_out, b_w_qkv, b_w_out,
           c_w_in, c_gn_g, c_w_out, f_w13, f_w2, ln1_g, ln1_b, ln2_g, ln2_b):
    bsz, seq, d = x.shape
    depth = f_w13.shape[0]
    alpha = (2 * depth) ** 0.25
    h = x.reshape(bsz * seq, d)
    hb = h
    for i in range(depth):
        kind, j = i % N_MIXERS, i // N_MIXERS
        if kind == 0:
            o, w_out = _gdn_layer(hb, bsz, seq, a_w_in[j], a_conv[j], a_a_log[j], a_dt_bias[j],
                                  a_norm_g[j], a_w_out[j])
        elif kind == 1:
            o, w_out = _moba_layer(hb, bsz, seq, b_w_qkv[j], b_w_out[j])
        else:
            o, w_out = _ret_layer(hb, bsz, seq, c_w_in[j], c_gn_g[j], c_w_out[j])
        h, hb = _block_tail(o, w_out, h, ln1_g[i], ln1_b[i], f_w13[i].astype(BF16), f_w2[i].astype(BF16),
                            ln2_g[i], ln2_b[i], alpha, 512, 256)
    return h.reshape(bsz, seq, d)
```

```python
import functools
import math

import jax
import jax.numpy as jnp
from jax import lax
from jax.experimental import pallas as pl
from jax.experimental.pallas import tpu as pltpu

F32 = jnp.float32
BF16 = jnp.bfloat16

N_MIXERS = 3
LANES = 128
VMEM_LIMIT = 56 * 1024 * 1024

GDN_DK = 128
GDN_CHUNK = 64
GDN_SUPER = 128
GDN_CONV = 4
GDN_WAVE = 4
MOBA_DH = 128
MOBA_BLOCK = 256
MOBA_TOPK = 3
MOBA_GROUP = 4
ONES_ROWS = 16
ROPE_THETA = 500000.0
ROPE_DIMS = MOBA_DH // 4
RET_DK = 256
RET_DV = 512
RET_TILE = 256
XPOS_BASE = 10000.0
LN_EPS = 1e-5
NORM_EPS = 1e-6
NEG_INF = -1e30

NT_DIMS = (((1,), (1,)), ((), ()))
TN_DIMS = (((0,), (0,)), ((), ()))


def _params(*semantics):
    return pltpu.CompilerParams(dimension_semantics=semantics, vmem_limit_bytes=VMEM_LIMIT)


def _silu(x):
    return x * jax.nn.sigmoid(x)


def _proj_kernel(x_ref, w_ref, o_ref):
    x = x_ref[...].astype(BF16)
    o_ref[...] = jnp.dot(x, w_ref[...], preferred_element_type=F32).astype(o_ref.dtype)


def _proj(x2d, w, out_dtype, tm, tn):
    m, k = x2d.shape
    n = w.shape[1]
    return pl.pallas_call(
        _proj_kernel,
        grid=(m // tm, n // tn),
        in_specs=[pl.BlockSpec((tm, k), lambda i, j: (i, 0)),
                  pl.BlockSpec((k, tn), lambda i, j: (0, j))],
        out_specs=pl.BlockSpec((tm, tn), lambda i, j: (i, j)),
        out_shape=jax.ShapeDtypeStruct((m, n), out_dtype),
        compiler_params=_params("parallel", "parallel"),
        name="proj",
    )(x2d, w)


def _proj_rope_partial_kernel(x_ref, w_ref, c_ref, s1_ref, s2_ref, o_ref, *, half, scale):
    x = x_ref[...].astype(BF16)
    acc = jnp.dot(x, w_ref[...], preferred_element_type=F32)
    tn = acc.shape[1]
    reps = tn // LANES
    c = jnp.tile(c_ref[...], (1, reps))
    s1 = jnp.tile(s1_ref[...], (1, reps))
    s2 = jnp.tile(s2_ref[...], (1, reps))
    out = acc * c + pltpu.roll(acc, tn - half, 1) * s1 + pltpu.roll(acc, half, 1) * s2
    o_ref[...] = (out * scale).astype(o_ref.dtype)


def _proj_rope_partial(x2d, w, tabs, seq, scale, tm, tn):
    m, k = x2d.shape
    n = w.shape[1]
    c, s1, s2 = tabs
    half = ROPE_DIMS // 2
    tpos = seq // tm
    tab_spec = pl.BlockSpec((tm, LANES), lambda i, j: (i % tpos, 0))
    return pl.pallas_call(
        functools.partial(_proj_rope_partial_kernel, half=half, scale=scale),
        grid=(m // tm, n // tn),
        in_specs=[pl.BlockSpec((tm, k), lambda i, j: (i, 0)),
                  pl.BlockSpec((k, tn), lambda i, j: (0, j)),
                  tab_spec, tab_spec, tab_spec],
        out_specs=pl.BlockSpec((tm, tn), lambda i, j: (i, j)),
        out_shape=jax.ShapeDtypeStruct((m, n), BF16),
        compiler_params=_params("parallel", "parallel"),
        name="proj_rope_partial",
    )(x2d, w, c, s1, s2)


def _proj_rope_full_kernel(x_ref, w_ref, cos_ref, sin_ref, o_ref, *, dk, scale):
    x = x_ref[...].astype(BF16)
    acc = jnp.dot(x, w_ref[...], preferred_element_type=F32)
    cos = cos_ref[...]
    sin = sin_ref[...]
    half = dk // 2
    for hh in range(acc.shape[1] // dk):
        a = acc[:, hh * dk: hh * dk + half]
        b = acc[:, hh * dk + half: (hh + 1) * dk]
        o_ref[:, hh * dk: hh * dk + half] = ((a * cos - b * sin) * scale).astype(o_ref.dtype)
        o_ref[:, hh * dk + half: (hh + 1) * dk] = ((a * sin + b * cos) * scale).astype(o_ref.dtype)


def _proj_rope_full(x2d, w, cos, sin, seq, dk, scale, tm, tn):
    m, k = x2d.shape
    n = w.shape[1]
    tpos = seq // tm
    tab_spec = pl.BlockSpec((tm, dk // 2), lambda i, j: (i % tpos, 0))
    return pl.pallas_call(
        functools.partial(_proj_rope_full_kernel, dk=dk, scale=scale),
        grid=(m // tm, n // tn),
        in_specs=[pl.BlockSpec((tm, k), lambda i, j: (i, 0)),
                  pl.BlockSpec((k, tn), lambda i, j: (0, j)),
                  tab_spec, tab_spec],
        out_specs=pl.BlockSpec((tm, tn), lambda i, j: (i, j)),
        out_shape=jax.ShapeDtypeStruct((m, n), BF16),
        compiler_params=_params("parallel", "parallel"),
        name="proj_rope_full",
    )(x2d, w, cos, sin)


def _layer_norm(y, g, b):
    mu = jnp.mean(y, axis=-1, keepdims=True)
    d = y - mu
    var = jnp.mean(d * d, axis=-1, keepdims=True)
    return d * lax.rsqrt(var + LN_EPS) * g + b


def _block_tail_kernel(o_ref, wo_ref, h_ref, g1_ref, b1_ref, w13_ref, w2_ref, g2_ref, b2_ref,
                       out_ref, outb_ref, *, alpha, dff, fc):
    mix = jnp.dot(o_ref[...], wo_ref[...], preferred_element_type=F32)
    h1 = _layer_norm(alpha * h_ref[...] + mix, g1_ref[...], b1_ref[...])
    xb = h1.astype(BF16)
    acc = alpha * h1
    for c in range(dff // fc):
        gate = jnp.dot(xb, w13_ref[:, c * fc:(c + 1) * fc], preferred_element_type=F32)
        up = jnp.dot(xb, w13_ref[:, dff + c * fc: dff + (c + 1) * fc], preferred_element_type=F32)
        act = (_silu(gate) * up).astype(BF16)
        acc = acc + jnp.dot(act, w2_ref[c * fc:(c + 1) * fc, :], preferred_element_type=F32)
    out = _layer_norm(acc, g2_ref[...], b2_ref[...])
    out_ref[...] = out
    outb_ref[...] = out.astype(BF16)


def _block_tail(o2d, wo, h2d, g1, b1, w13, w2, g2, b2, alpha, tm, fc):
    m, kin = o2d.shape
    d = wo.shape[1]
    dff = w2.shape[0]
    once = pl.Buffered(1)
    vec = pl.BlockSpec((1, d), lambda i: (0, 0))
    rows = pl.BlockSpec((tm, d), lambda i: (i, 0))
    return pl.pallas_call(
        functools.partial(_block_tail_kernel, alpha=alpha, dff=dff, fc=fc),
        grid=(m // tm,),
        in_specs=[pl.BlockSpec((tm, kin), lambda i: (i, 0)),
                  pl.BlockSpec((kin, d), lambda i: (0, 0), pipeline_mode=once),
                  rows, vec, vec,
                  pl.BlockSpec((d, 2 * dff), lambda i: (0, 0), pipeline_mode=once),
                  pl.BlockSpec((dff, d), lambda i: (0, 0), pipeline_mode=once),
                  vec, vec],
        out_specs=[rows, rows],
        out_shape=[jax.ShapeDtypeStruct((m, d), F32), jax.ShapeDtypeStruct((m, d), BF16)],
        compiler_params=_params("parallel"),
        name="block_tail",
    )(o2d, wo, h2d, g1.reshape(1, d), b1.reshape(1, d), w13, w2, g2.reshape(1, d), b2.reshape(1, d))


def _gdn_gates_kernel(g_ref, alog_ref, dtb_ref, o_ref, *, c, nheads):
    gt = g_ref[0]
    tg = gt.shape[0]
    sp_in = gt + dtb_ref[...]
    softplus = jnp.maximum(sp_in, 0.0) + jnp.log1p(jnp.exp(-jnp.abs(sp_in)))
    g = -jnp.exp(alog_ref[...]) * softplus
    ri = lax.broadcasted_iota(jnp.int32, (tg, tg), 0)
    ci = lax.broadcasted_iota(jnp.int32, (tg, tg), 1)
    tri = jnp.where(((ri // c) == (ci // c)) & (ri >= ci), 1.0, 0.0).astype(BF16)
    g1 = g.astype(BF16)
    r1 = g - g1.astype(F32)
    g2 = r1.astype(BF16)
    g3 = (r1 - g2.astype(F32)).astype(BF16)
    gc = (jnp.dot(tri, g1, preferred_element_type=F32) + jnp.dot(tri, g2, preferred_element_type=F32)
          + jnp.dot(tri, g3, preferred_element_type=F32))
    lane = lax.broadcasted_iota(jnp.int32, (tg, LANES), 1)
    o_ref[0] = jnp.where(lane < nheads, jax.nn.sigmoid(gt), gc)


def _gdn_gates(gates, a_log, dt_bias, tg):
    bsz, seq, _ = gates.shape
    nh = a_log.shape[0]
    pad = LANES - 2 * nh
    alog_row = jnp.concatenate([jnp.zeros((nh,), F32), a_log.astype(F32), jnp.zeros((pad,), F32)]).reshape(1, LANES)
    dtb_row = jnp.concatenate([jnp.zeros((nh,), F32), dt_bias.astype(F32), jnp.zeros((pad,), F32)]).reshape(1, LANES)
    tile = pl.BlockSpec((1, tg, LANES), lambda b, t: (b, t, 0))
    row = pl.BlockSpec((1, LANES), lambda b, t: (0, 0))
    return pl.pallas_call(
        functools.partial(_gdn_gates_kernel, c=GDN_CHUNK, nheads=nh),
        grid=(bsz, seq // tg),
        in_specs=[tile, row, row],
        out_specs=tile,
        out_shape=jax.ShapeDtypeStruct(gates.shape, F32),
        compiler_params=_params("parallel", "parallel"),
        name="gdn_gates",
    )(gates, alog_row, dtb_row)


def _gdn_kernel(q_ref, k_ref, v_ref, z_ref, qp_ref, kp_ref, vp_ref, gates_ref,
                wq_ref, wk_ref, wv_ref, ng_ref, o_ref, state_ref,
                *, tt, nheads, hp, wave):
    hg = pl.program_id(1)
    t = pl.program_id(2)
    c = GDN_CHUNK
    sc = GDN_SUPER
    ns = tt // sc
    nc = tt // c
    dk = GDN_DK

    @pl.when(t == 0)
    def _():
        state_ref[...] = jnp.zeros_like(state_ref)

    row16 = lax.broadcasted_iota(jnp.int32, (16, LANES), 0)

    def conv_silu(x, xp, w):
        x = x.astype(F32)
        xp = jnp.where(t > 0, xp.astype(F32), 0.0)
        y = x * w[GDN_CONV - 1:GDN_CONV, :]
        for s in range(1, GDN_CONV):
            xs = pltpu.roll(x, s, 0)
            head = jnp.where(row16 < s, pltpu.roll(xp, s, 0), xs[:16])
            xs = jnp.concatenate([head, xs[16:]], axis=0)
            y = y + xs * w[GDN_CONV - 1 - s:GDN_CONV - s, :]
        return _silu(y)

    def l2n(x):
        return x * lax.rsqrt(jnp.sum(x * x, axis=-1, keepdims=True) + NORM_EPS)

    gb_all = gates_ref[0]
    lane = lax.broadcasted_iota(jnp.int32, (tt, LANES), 1)

    ri = lax.broadcasted_iota(jnp.int32, (sc, sc), 0)
    ci = lax.broadcasted_iota(jnp.int32, (sc, sc), 1)
    same = (ri // c) == (ci // c)
    tril = same & (ri >= ci)
    strict = same & (ri > ci)
    eye = (ri == ci).astype(F32)

    dm, lhs, k_s, rhs, qd_u, kd_u, cd = {}, {}, {}, {}, {}, {}, {}
    o_loc, q_eff, kuw = {}, {}, {}

    def prepare(hh):
        ln = slice(hh * dk, (hh + 1) * dk)
        hd = hg * hp + hh
        qn = l2n(conv_silu(q_ref[0, :, ln], qp_ref[0, :, ln], wq_ref[:, ln])) * (dk ** -0.5)
        kn = l2n(conv_silu(k_ref[0, :, ln], kp_ref[0, :, ln], wk_ref[:, ln]))
        vv = conv_silu(v_ref[0, :, ln], vp_ref[0, :, ln], wv_ref[:, ln])
        beta = jnp.sum(jnp.where(lane == hd, gb_all, 0.0), axis=-1, keepdims=True)
        gcol = jnp.sum(jnp.where(lane == nheads + hd, gb_all, 0.0), axis=-1, keepdims=True)
        glast = jnp.broadcast_to(gcol.reshape(nc, c, 1)[:, c - 1:c, :], (nc, c, 1)).reshape(tt, 1)
        eg = jnp.exp(gcol)
        kb = kn * beta
        kbg = kb * eg
        qd = qn * eg
        kd = kn * jnp.exp(glast - gcol)
        vb = vv * beta
        for sidx in range(ns):
            r = slice(sidx * sc, (sidx + 1) * sc)
            u = (hh, sidx)
            gb = jnp.broadcast_to(gcol[r], (sc, sc))
            dm[u] = jnp.exp(jnp.where(tril, gb - gb.T, NEG_INF))
            k_s[u] = kn[r].astype(BF16)
            lhs[u] = jnp.concatenate([kb[r], qn[r]], axis=0).astype(BF16)
            rhs[u] = jnp.concatenate([vb[r], kbg[r]], axis=1).astype(BF16)
            qd_u[u] = qd[r]
            kd_u[u] = kd[r].astype(BF16)
            for half in range(sc // c):
                row = sidx * sc + half * c
                cd[hh, sidx * (sc // c) + half] = jnp.exp(glast[row:row + 1])

    def wy_transform(units):
        kq = {u: lax.dot_general(lhs[u], k_s[u], NT_DIMS, preferred_element_type=F32) for u in units}
        attn = {u: (kq[u][sc:] * dm[u]).astype(BF16) for u in units}
        low = {u: jnp.where(strict, kq[u][:sc] * dm[u], 0.0) for u in units}
        pw = {u: low[u].astype(BF16) for u in units}
        tinv = {u: eye - low[u] for u in units}
        span = 2
        while span < c:
            pw = {u: jnp.dot(pw[u], pw[u], preferred_element_type=F32).astype(BF16) for u in units}
            tinv = {u: tinv[u] + jnp.dot(tinv[u].astype(BF16), pw[u], preferred_element_type=F32)
                    for u in units}
            span *= 2
        uw_b = {u: jnp.dot(tinv[u].astype(BF16), rhs[u], preferred_element_type=F32).astype(BF16)
                for u in units}
        auw = {u: jnp.dot(attn[u], uw_b[u], preferred_element_type=F32) for u in units}
        for u in units:
            hh, sidx = u
            qe = (qd_u[u] - auw[u][:, dk:]).astype(BF16)
            for half in range(sc // c):
                a = slice(half * c, (half + 1) * c)
                ch = sidx * (sc // c) + half
                o_loc[hh, ch] = auw[u][a, :dk]
                q_eff[hh, ch] = qe[a]
                kuw[hh, ch] = lax.dot_general(kd_u[u][a], uw_b[u][a], TN_DIMS,
                                              preferred_element_type=F32)

    for w0 in range(0, hp, wave):
        heads = range(w0, min(w0 + wave, hp))
        for hh in heads:
            prepare(hh)
        wy_transform([(hh, sidx) for hh in heads for sidx in range(ns)])

    o_parts = [[] for _ in range(hp)]
    for ch in range(nc):
        for hh in range(hp):
            st = state_ref[hh]
            lhs2 = jnp.concatenate([kuw[hh, ch][:, dk:].astype(BF16), q_eff[hh, ch]], axis=0)
            res = jnp.dot(lhs2, st.astype(BF16), preferred_element_type=F32)
            o_parts[hh].append(o_loc[hh, ch] + res[dk:])
            state_ref[hh] = st * cd[hh, ch] + kuw[hh, ch][:, :dk] - res[:dk]

    for hh in range(hp):
        ln = slice(hh * dk, (hh + 1) * dk)
        o = jnp.concatenate(o_parts[hh], axis=0)
        o = o * lax.rsqrt(jnp.mean(o * o, axis=-1, keepdims=True) + NORM_EPS) * ng_ref[...]
        o = o * _silu(z_ref[0, :, ln].astype(F32))
        o_ref[0, :, ln] = o.astype(o_ref.dtype)


def _gdn_core(qkvz, gates, w_conv, norm_g, tt, hp):
    bsz, seq, width = qkvz.shape
    nh = width // (4 * GDN_DK)
    ng = nh // hp
    hw = hp * GDN_DK
    prev = tt // 16

    def cur(off):
        return pl.BlockSpec((1, tt, hw), lambda b, h, t: (b, t, off + h))

    def prv(off):
        return pl.BlockSpec((1, 16, hw), lambda b, h, t: (b, jnp.maximum(t * prev - 1, 0), off + h))

    def cw(off):
        return pl.BlockSpec((GDN_CONV, hw), lambda b, h, t: (0, off + h))

    row = pl.BlockSpec((1, LANES), lambda b, h, t: (0, 0))
    return pl.pallas_call(
        functools.partial(_gdn_kernel, tt=tt, nheads=nh, hp=hp, wave=GDN_WAVE),
        grid=(bsz, ng, seq // tt),
        in_specs=[cur(0), cur(ng), cur(2 * ng), cur(3 * ng),
                  prv(0), prv(ng), prv(2 * ng),
                  pl.BlockSpec((1, tt, LANES), lambda b, h, t: (b, t, 0)),
                  cw(0), cw(ng), cw(2 * ng), row],
        out_specs=pl.BlockSpec((1, tt, hw), lambda b, h, t: (b, t, h)),
        out_shape=jax.ShapeDtypeStruct((bsz, seq, nh * GDN_DK), BF16),
        scratch_shapes=[pltpu.VMEM((hp, GDN_DK, GDN_DK), F32)],
        compiler_params=_params("parallel", "parallel", "arbitrary"),
        name="gdn_core",
    )(qkvz, qkvz, qkvz, qkvz, qkvz, qkvz, qkvz, gates,
      w_conv, w_conv, w_conv, norm_g.astype(F32).reshape(1, LANES))


def _gdn_layer(h2d, bsz, seq, w_in, w_conv, a_log, dt_bias, norm_g, w_out):
    nh = a_log.shape[0]
    main = 4 * nh * GDN_DK
    w_main = w_in[:, :main].astype(BF16)
    w_gate = jnp.pad(w_in[:, main:], ((0, 0), (0, LANES - 2 * nh))).astype(BF16)
    qkvz = _proj(h2d, w_main, BF16, 1024, 2048).reshape(bsz, seq, main)
    gates = _proj(h2d, w_gate, F32, 1024, LANES).reshape(bsz, seq, LANES)
    gates = _gdn_gates(gates, a_log, dt_bias, tg=512)
    o = _gdn_core(qkvz, gates, w_conv.astype(F32), norm_g, tt=512, hp=4)
    return o.reshape(bsz * seq, nh * GDN_DK), w_out.astype(BF16)


def _moba_kernel(q_ref, k_ref, v_ref, o_ref, kmh_ref, kml_ref, vt_ref, *, nb, bs, grp, hp):
    i = pl.program_id(2)
    dh = MOBA_DH
    gw = grp * bs
    heads = range(hp)

    @pl.when(i == 0)
    def _():
        for hh in heads:
            ln = slice(hh * dh, (hh + 1) * dh)
            kf = k_ref[0, :, ln].astype(F32).reshape(nb, bs, dh)
            km = jnp.sum(kf, axis=1) * (1.0 / bs)
            hi = km.astype(BF16)
            kmh_ref[hh] = hi
            kml_ref[hh] = (km - hi.astype(F32)).astype(BF16)

            def tr(j, carry):
                vj = v_ref[0, pl.ds(pl.multiple_of(j * gw, gw), gw), ln].astype(F32)
                vjt = vj.T.astype(BF16)
                for u in range(grp):
                    vt_ref[hh, j * grp + u, :dh, :] = vjt[:, u * bs:(u + 1) * bs]
                    vt_ref[hh, j * grp + u, dh:, :] = jnp.ones((ONES_ROWS, bs), BF16)
                return carry
            lax.fori_loop(0, nb // grp, tr, 0)

    q = [q_ref[0, :, hh * dh:(hh + 1) * dh] for hh in heads]
    own = pl.multiple_of(i * bs, bs)
    kpos = lax.broadcasted_iota(jnp.int32, (bs, bs), 0)
    qpos = lax.broadcasted_iota(jnp.int32, (bs, bs), 1)

    def select_bias(hh):
        gate = (lax.dot_general(kmh_ref[hh], q[hh], NT_DIMS, preferred_element_type=F32) +
                lax.dot_general(kml_ref[hh], q[hh], NT_DIMS, preferred_element_type=F32))
        blk = lax.broadcasted_iota(jnp.int32, (nb, bs), 0)
        past = blk < i
        g = jnp.where(past, gate, -jnp.inf)
        sel = jnp.zeros((nb, bs), F32)
        for _ in range(MOBA_TOPK):
            mx = jnp.max(g, axis=0, keepdims=True)
            idx = jnp.min(jnp.where(g == mx, blk, nb), axis=0, keepdims=True)
            hit = blk == idx
            sel = jnp.where(hit, 1.0, sel)
            g = jnp.where(hit, -jnp.inf, g)
        return jnp.where(past & (sel > 0.0), 0.0, NEG_INF)

    def scores(hh, row0):
        return lax.dot_general(k_ref[0, pl.ds(row0, bs), hh * dh:(hh + 1) * dh], q[hh], NT_DIMS,
                               preferred_element_type=F32)

    def attend(ngroups):
        s_own = [jnp.where(kpos <= qpos, scores(hh, own), NEG_INF) for hh in heads]
        s_next = [[scores(hh, u * bs) for u in range(grp)] if ngroups else [] for hh in heads]
        bias = [select_bias(hh) if ngroups else None for hh in heads]
        m = [jnp.max(s_own[hh], axis=0, keepdims=True) for hh in heads]
        p = [jnp.exp2(s_own[hh] - m[hh]).astype(BF16) for hh in heads]
        acc = [jnp.dot(vt_ref[hh, i], p[hh], preferred_element_type=F32) for hh in heads]
        for gi in range(ngroups):
            s_cur = s_next
            s_next = [[scores(hh, (gi + 1) * gw + u * bs) for u in range(grp)] if gi + 1 < ngroups else []
                      for hh in heads]
            brow = [[bias[hh][gi * grp + u:gi * grp + u + 1] for u in range(grp)] for hh in heads]
            m_new = list(m)
            for u in range(grp):
                for hh in heads:
                    m_new[hh] = jnp.maximum(m_new[hh],
                                            jnp.max(s_cur[hh][u], axis=0, keepdims=True) + brow[hh][u])
            acc = [jnp.exp2(m[hh] - m_new[hh]) * acc[hh] for hh in heads]
            for u in range(grp):
                for hh in heads:
                    pu = jnp.exp2((s_cur[hh][u] - (m_new[hh] - brow[hh][u])).astype(BF16))
                    acc[hh] = acc[hh] + jnp.dot(vt_ref[hh, gi * grp + u], pu, preferred_element_type=F32)
            m = m_new
        for hh in heads:
            o_ref[0, :, hh * dh:(hh + 1) * dh] = (acc[hh][:dh] / acc[hh][dh:dh + 1]).T.astype(o_ref.dtype)

    needed = (i + grp - 1) // grp
    for ngroups in range(nb // grp + 1):
        pl.when(needed == ngroups)(functools.partial(attend, ngroups))


def _moba_core(q, k, v, hp):
    bsz, seq, width = q.shape
    nh = width // MOBA_DH
    bs = MOBA_BLOCK
    nb = seq // bs
    grp = math.gcd(nb, MOBA_GROUP)
    hw = hp * MOBA_DH
    full = pl.BlockSpec((1, seq, hw), lambda b, h, i: (b, 0, h))
    blk = pl.BlockSpec((1, bs, hw), lambda b, h, i: (b, i, h))
    return pl.pallas_call(
        functools.partial(_moba_kernel, nb=nb, bs=bs, grp=grp, hp=hp),
        grid=(bsz, nh // hp, nb),
        in_specs=[blk, full, full],
        out_specs=blk,
        out_shape=jax.ShapeDtypeStruct((bsz, seq, width), BF16),
        scratch_shapes=[pltpu.VMEM((hp, nb, MOBA_DH), BF16), pltpu.VMEM((hp, nb, MOBA_DH), BF16),
                        pltpu.VMEM((hp, nb, MOBA_DH + ONES_ROWS, bs), BF16)],
        compiler_params=_params("parallel", "parallel", "arbitrary"),
        name="moba_core",
    )(q, k, v)


def _moba_rope_tables(seq):
    half = ROPE_DIMS // 2
    inv_freq = jnp.power(ROPE_THETA, -jnp.arange(half, dtype=F32) / half)
    ang = jnp.arange(seq, dtype=F32)[:, None] * inv_freq[None, :]
    cos, sin = jnp.cos(ang), jnp.sin(ang)
    rest = LANES - 2 * half
    c = jnp.concatenate([cos, cos, jnp.ones((seq, rest), F32)], axis=1)
    s1 = jnp.concatenate([-sin, jnp.zeros((seq, LANES - half), F32)], axis=1)
    s2 = jnp.concatenate([jnp.zeros((seq, half), F32), sin, jnp.zeros((seq, rest), F32)], axis=1)
    return c, s1, s2


def _moba_layer(h2d, bsz, seq, w_qkv, w_out):
    width = w_qkv.shape[1] // 3
    nh = width // MOBA_DH
    assert seq % MOBA_BLOCK == 0
    tabs = _moba_rope_tables(seq)
    wq, wk, wv = (w_qkv[:, n * width:(n + 1) * width].astype(BF16) for n in range(3))
    q = _proj_rope_partial(h2d, wq, tabs, seq, MOBA_DH ** -0.5 * math.log2(math.e), 1024, width)
    k = _proj_rope_partial(h2d, wk, tabs, seq, 1.0, 1024, width)
    v = _proj(h2d, wv, BF16, 1024, width)
    shp = (bsz, seq, width)
    o = _moba_core(q.reshape(shp), k.reshape(shp), v.reshape(shp), hp=2)
    return o.reshape(bsz * seq, nh * MOBA_DH), w_out.astype(BF16)


def _ret_kernel(lg_ref, q_ref, k_ref, v_ref, gate_ref, gn_ref, o_ref, state_ref, dmat_ref, *, c, hp):
    t = pl.program_id(2)
    dk, dv = RET_DK, RET_DV
    heads = range(hp)
    lg = [lg_ref[hh][:, :1] for hh in heads]

    @pl.when(t == 0)
    def _():
        state_ref[...] = jnp.zeros_like(state_ref)
        ri = lax.broadcasted_iota(jnp.int32, (c, c), 0)
        ci = lax.broadcasted_iota(jnp.int32, (c, c), 1)
        tril = ri >= ci
        rel = jnp.where(tril, ri - ci, 0).astype(F32)
        for hh in heads:
            dmat_ref[hh] = jnp.where(tril, jnp.exp(lg[hh] * rel), 0.0)

    pos = lax.broadcasted_iota(jnp.int32, (c, 1), 0).astype(F32)
    q = [q_ref[0, :, hh * dk:(hh + 1) * dk] for hh in heads]
    k = [k_ref[0, :, hh * dk:(hh + 1) * dk] for hh in heads]
    v = [v_ref[0, :, hh * dv:(hh + 1) * dv] for hh in heads]
    st = [state_ref[hh] for hh in heads]
    inner = [lax.dot_general(q[hh], k[hh], NT_DIMS, preferred_element_type=F32) for hh in heads]
    qd = [(q[hh].astype(F32) * jnp.exp(lg[hh] * (pos + 1.0))).astype(BF16) for hh in heads]
    kd = [(k[hh].astype(F32) * jnp.exp(lg[hh] * (c - 1.0 - pos))).astype(BF16) for hh in heads]
    cross = [jnp.dot(qd[hh], st[hh].astype(BF16), preferred_element_type=F32) for hh in heads]
    kv = [lax.dot_general(kd[hh], v[hh], TN_DIMS, preferred_element_type=F32) for hh in heads]
    intra = [jnp.dot((inner[hh] * dmat_ref[hh]).astype(BF16), v[hh], preferred_element_type=F32)
             for hh in heads]
    for hh in heads:
        state_ref[hh] = st[hh] * jnp.exp(lg[hh] * c) + kv[hh]
        o = intra[hh] + cross[hh]
        mu = jnp.mean(o, axis=-1, keepdims=True)
        d = o - mu
        var = jnp.mean(d * d, axis=-1, keepdims=True)
        o = d * lax.rsqrt(var + LN_EPS) * gn_ref[:, hh * dv:(hh + 1) * dv]
        gate = gate_ref[0, :, hh * dv:(hh + 1) * dv].astype(F32)
        o_ref[0, :, hh * dv:(hh + 1) * dv] = (_silu(gate) * o).astype(o_ref.dtype)


def _ret_core(q, k, vg, gn_g, log_gamma, hp):
    bsz, seq, _ = q.shape
    nh = log_gamma.shape[0]
    ng = nh // hp
    c = RET_TILE
    lg = jnp.broadcast_to(log_gamma.reshape(nh, 1, 1), (nh, 1, LANES)).astype(F32)
    qk_spec = pl.BlockSpec((1, c, hp * RET_DK), lambda b, h, t: (b, t, h))
    v_spec = pl.BlockSpec((1, c, hp * RET_DV), lambda b, h, t: (b, t, h))
    gate_spec = pl.BlockSpec((1, c, hp * RET_DV), lambda b, h, t: (b, t, ng + h))
    return pl.pallas_call(
        functools.partial(_ret_kernel, c=c, hp=hp),
        grid=(bsz, ng, seq // c),
        in_specs=[pl.BlockSpec((hp, 1, LANES), lambda b, h, t: (h, 0, 0)),
                  qk_spec, qk_spec, v_spec, gate_spec,
                  pl.BlockSpec((1, hp * RET_DV), lambda b, h, t: (0, h))],
        out_specs=v_spec,
        out_shape=jax.ShapeDtypeStruct((bsz, seq, nh * RET_DV), BF16),
        scratch_shapes=[pltpu.VMEM((hp, RET_DK, RET_DV), F32), pltpu.VMEM((hp, c, c), F32)],
        compiler_params=_params("parallel", "parallel", "arbitrary"),
        name="ret_core",
    )(lg, q, k, vg, vg, gn_g.astype(F32).reshape(1, nh * RET_DV))


def _ret_layer(h2d, bsz, seq, w_in, gn_g, w_out):
    nh = gn_g.shape[0] // RET_DV
    qw = nh * RET_DK
    vw = nh * RET_DV
    assert seq % RET_TILE == 0
    inv_freq = jnp.power(XPOS_BASE, -jnp.linspace(0.0, 1.0, RET_DK // 2, dtype=F32))
    ang = jnp.arange(seq, dtype=F32)[:, None] * inv_freq[None, :]
    cos, sin = jnp.cos(ang), jnp.sin(ang)
    log_gamma = jnp.log1p(-jnp.exp2(-5.0 - jnp.arange(nh, dtype=F32)))
    wb = w_in.astype(BF16)
    q = _proj_rope_full(h2d, wb[:, :qw], cos, sin, seq, RET_DK, 1.0, 1024, qw)
    k = _proj_rope_full(h2d, wb[:, qw:2 * qw], cos, sin, seq, RET_DK, RET_DK ** -0.5, 1024, qw)
    vg = _proj(h2d, wb[:, 2 * qw:], BF16, 1024, 2048)
    o = _ret_core(q.reshape(bsz, seq, qw), k.reshape(bsz, seq, qw), vg.reshape(bsz, seq, 2 * vw),
                  gn_g, log_gamma, hp=4)
    return o.reshape(bsz * seq, vw), w_out.astype(BF16)


def kernel(x, a_w_in, a_conv, a_a_log, a_dt_bias, a_norm_g, a_w_out, b_w_qkv, b_w_out,
           c_w_in, c_gn_g, c_w_out, f_w13, f_w2, ln1_g, ln1_b, ln2_g, ln2_b):
    bsz, seq, d = x.shape
    depth = f_w13.shape[0]
    alpha = (2 * depth) ** 0.25
    h = x.reshape(bsz * seq, d)
    hb = h
    for i in range(depth):
        kind, j = i % N_MIXERS, i // N_MIXERS
        if kind == 0:
            o, w_out = _gdn_layer(hb, bsz, seq, a_w_in[j], a_conv[j], a_a_log[j], a_dt_bias[j],
                                  a_norm_g[j], a_w_out[j])
        elif kind == 1:
            o, w_out = _moba_layer(hb, bsz, seq, b_w_qkv[j], b_w_out[j])
        else:
            o, w_out = _ret_layer(hb, bsz, seq, c_w_in[j], c_gn_g[j], c_w_out[j])
        h, hb = _block_tail(o, w_out, h, ln1_g[i], ln1_b[i], f_w13[i].astype(BF16), f_w2[i].astype(BF16),
                            ln2_g[i], ln2_b[i], alpha, 512, 256)
    return h.reshape(bsz, seq, d)
```

```python
import functools
import math

import jax
import jax.numpy as jnp
from jax import lax
from jax.experimental import pallas as pl
from jax.experimental.pallas import tpu as pltpu

F32 = jnp.float32
BF16 = jnp.bfloat16

N_MIXERS = 3
LANES = 128
VMEM_LIMIT = 56 * 1024 * 1024

GDN_DK = 128
GDN_CHUNK = 64
GDN_SUPER = 128
GDN_CONV = 4
GDN_WAVE = 4
MOBA_DH = 128
MOBA_BLOCK = 256
MOBA_TOPK = 3
MOBA_GROUP = 4
ONES_ROWS = 16
ROPE_THETA = 500000.0
ROPE_DIMS = MOBA_DH // 4
RET_DK = 256
RET_DV = 512
RET_TILE = 256
XPOS_BASE = 10000.0
LN_EPS = 1e-5
NORM_EPS = 1e-6
NEG_INF = -1e30

NT_DIMS = (((1,), (1,)), ((), ()))
TN_DIMS = (((0,), (0,)), ((), ()))


def _params(*semantics):
    return pltpu.CompilerParams(dimension_semantics=semantics, vmem_limit_bytes=VMEM_LIMIT)


def _silu(x):
    return x * jax.nn.sigmoid(x)


def _proj_kernel(x_ref, w_ref, o_ref):
    x = x_ref[...].astype(BF16)
    o_ref[...] = jnp.dot(x, w_ref[...], preferred_element_type=F32).astype(o_ref.dtype)


def _proj(x2d, w, out_dtype, tm, tn):
    m, k = x2d.shape
    n = w.shape[1]
    return pl.pallas_call(
        _proj_kernel,
        grid=(m // tm, n // tn),
        in_specs=[pl.BlockSpec((tm, k), lambda i, j: (i, 0)),
                  pl.BlockSpec((k, tn), lambda i, j: (0, j))],
        out_specs=pl.BlockSpec((tm, tn), lambda i, j: (i, j)),
        out_shape=jax.ShapeDtypeStruct((m, n), out_dtype),
        compiler_params=_params("parallel", "parallel"),
        name="proj",
    )(x2d, w)


def _proj_rope_partial_kernel(x_ref, w_ref, c_ref, s1_ref, s2_ref, o_ref, *, half, scale):
    x = x_ref[...].astype(BF16)
    acc = jnp.dot(x, w_ref[...], preferred_element_type=F32)
    tn = acc.shape[1]
    reps = tn // LANES
    c = jnp.tile(c_ref[...], (1, reps))
    s1 = jnp.tile(s1_ref[...], (1, reps))
    s2 = jnp.tile(s2_ref[...], (1, reps))
    out = acc * c + pltpu.roll(acc, tn - half, 1) * s1 + pltpu.roll(acc, half, 1) * s2
    o_ref[...] = (out * scale).astype(o_ref.dtype)


def _proj_rope_partial(x2d, w, tabs, seq, scale, tm, tn):
    m, k = x2d.shape
    n = w.shape[1]
    c, s1, s2 = tabs
    half = ROPE_DIMS // 2
    tpos = seq // tm
    tab_spec = pl.BlockSpec((tm, LANES), lambda i, j: (i % tpos, 0))
    return pl.pallas_call(
        functools.partial(_proj_rope_partial_kernel, half=half, scale=scale),
        grid=(m // tm, n // tn),
        in_specs=[pl.BlockSpec((tm, k), lambda i, j: (i, 0)),
                  pl.BlockSpec((k, tn), lambda i, j: (0, j)),
                  tab_spec, tab_spec, tab_spec],
        out_specs=pl.BlockSpec((tm, tn), lambda i, j: (i, j)),
        out_shape=jax.ShapeDtypeStruct((m, n), BF16),
        compiler_params=_params("parallel", "parallel"),
        name="proj_rope_partial",
    )(x2d, w, c, s1, s2)


def _proj_rope_full_kernel(x_ref, w_ref, cos_ref, sin_ref, o_ref, *, dk, scale):
    x = x_ref[...].astype(BF16)
    acc = jnp.dot(x, w_ref[...], preferred_element_type=F32)
    cos = cos_ref[...]
    sin = sin_ref[...]
    half = dk // 2
    for hh in range(acc.shape[1] // dk):
        a = acc[:, hh * dk: hh * dk + half]
        b = acc[:, hh * dk + half: (hh + 1) * dk]
        o_ref[:, hh * dk: hh * dk + half] = ((a * cos - b * sin) * scale).astype(o_ref.dtype)
        o_ref[:, hh * dk + half: (hh + 1) * dk] = ((a * sin + b * cos) * scale).astype(o_ref.dtype)


def _proj_rope_full(x2d, w, cos, sin, seq, dk, scale, tm, tn):
    m, k = x2d.shape
    n = w.shape[1]
    tpos = seq // tm
    tab_spec = pl.BlockSpec((tm, dk // 2), lambda i, j: (i % tpos, 0))
    return pl.pallas_call(
        functools.partial(_proj_rope_full_kernel, dk=dk, scale=scale),
        grid=(m // tm, n // tn),
        in_specs=[pl.BlockSpec((tm, k), lambda i, j: (i, 0)),
                  pl.BlockSpec((k, tn), lambda i, j: (0, j)),
                  tab_spec, tab_spec],
        out_specs=pl.BlockSpec((tm, tn), lambda i, j: (i, j)),
        out_shape=jax.ShapeDtypeStruct((m, n), BF16),
        compiler_params=_params("parallel", "parallel"),
        name="proj_rope_full",
    )(x2d, w, cos, sin)


def _layer_norm(y, g, b):
    mu = jnp.mean(y, axis=-1, keepdims=True)
    d = y - mu
    var = jnp.mean(d * d, axis=-1, keepdims=True)
    return d * lax.rsqrt(var + LN_EPS) * g + b


def _block_tail_kernel(o_ref, wo_ref, h_ref, g1_ref, b1_ref, w13_ref, w2_ref, g2_ref, b2_ref,
                       out_ref, outb_ref, *, alpha, dff, fc):
    mix = jnp.dot(o_ref[...], wo_ref[...], preferred_element_type=F32)
    h1 = _layer_norm(alpha * h_ref[...] + mix, g1_ref[...], b1_ref[...])
    xb = h1.astype(BF16)
    acc = alpha * h1
    for c in range(dff // fc):
        gate = jnp.dot(xb, w13_ref[:, c * fc:(c + 1) * fc], preferred_element_type=F32)
        up = jnp.dot(xb, w13_ref[:, dff + c * fc: dff + (c + 1) * fc], preferred_element_type=F32)
        act = (_silu(gate) * up).astype(BF16)
        acc = acc + jnp.dot(act, w2_ref[c * fc:(c + 1) * fc, :], preferred_element_type=F32)
    out = _layer_norm(acc, g2_ref[...], b2_ref[...])
    out_ref[...] = out
    outb_ref[...] = out.astype(BF16)


def _block_tail(o2d, wo, h2d, g1, b1, w13, w2, g2, b2, alpha, tm, fc):
    m, kin = o2d.shape
    d = wo.shape[1]
    dff = w2.shape[0]
    once = pl.Buffered(1)
    vec = pl.BlockSpec((1, d), lambda i: (0, 0))
    rows = pl.BlockSpec((tm, d), lambda i: (i, 0))
    return pl.pallas_call(
        functools.partial(_block_tail_kernel, alpha=alpha, dff=dff, fc=fc),
        grid=(m // tm,),
        in_specs=[pl.BlockSpec((tm, kin), lambda i: (i, 0)),
                  pl.BlockSpec((kin, d), lambda i: (0, 0), pipeline_mode=once),
                  rows, vec, vec,
                  pl.BlockSpec((d, 2 * dff), lambda i: (0, 0), pipeline_mode=once),
                  pl.BlockSpec((dff, d), lambda i: (0, 0), pipeline_mode=once),
                  vec, vec],
        out_specs=[rows, rows],
        out_shape=[jax.ShapeDtypeStruct((m, d), F32), jax.ShapeDtypeStruct((m, d), BF16)],
        compiler_params=_params("parallel"),
        name="block_tail",
    )(o2d, wo, h2d, g1.reshape(1, d), b1.reshape(1, d), w13, w2, g2.reshape(1, d), b2.reshape(1, d))


def _gdn_gates_kernel(x_ref, w_ref, alog_ref, dtb_ref, o_ref, *, c, nheads):
    gt = jnp.dot(x_ref[0].astype(BF16), w_ref[...], preferred_element_type=F32)
    tg = gt.shape[0]
    sp_in = gt + dtb_ref[...]
    softplus = jnp.maximum(sp_in, 0.0) + jnp.log1p(jnp.exp(-jnp.abs(sp_in)))
    g = -jnp.exp(alog_ref[...]) * softplus
    ri = lax.broadcasted_iota(jnp.int32, (tg, tg), 0)
    ci = lax.broadcasted_iota(jnp.int32, (tg, tg), 1)
    tri = jnp.where(((ri // c) == (ci // c)) & (ri >= ci), 1.0, 0.0).astype(BF16)
    g1 = g.astype(BF16)
    r1 = g - g1.astype(F32)
    g2 = r1.astype(BF16)
    g3 = (r1 - g2.astype(F32)).astype(BF16)
    gc = (jnp.dot(tri, g1, preferred_element_type=F32) + jnp.dot(tri, g2, preferred_element_type=F32)
          + jnp.dot(tri, g3, preferred_element_type=F32))
    lane = lax.broadcasted_iota(jnp.int32, (tg, LANES), 1)
    o_ref[0] = jnp.where(lane < nheads, jax.nn.sigmoid(gt), gc)


def _gdn_gates(x3d, w_gate, a_log, dt_bias, tg):
    bsz, seq, d = x3d.shape
    nh = a_log.shape[0]
    pad = LANES - 2 * nh
    alog_row = jnp.concatenate([jnp.zeros((nh,), F32), a_log.astype(F32), jnp.zeros((pad,), F32)]).reshape(1, LANES)
    dtb_row = jnp.concatenate([jnp.zeros((nh,), F32), dt_bias.astype(F32), jnp.zeros((pad,), F32)]).reshape(1, LANES)
    row = pl.BlockSpec((1, LANES), lambda b, t: (0, 0))
    return pl.pallas_call(
        functools.partial(_gdn_gates_kernel, c=GDN_CHUNK, nheads=nh),
        grid=(bsz, seq // tg),
        in_specs=[pl.BlockSpec((1, tg, d), lambda b, t: (b, t, 0)),
                  pl.BlockSpec((d, LANES), lambda b, t: (0, 0)), row, row],
        out_specs=pl.BlockSpec((1, tg, LANES), lambda b, t: (b, t, 0)),
        out_shape=jax.ShapeDtypeStruct((bsz, seq, LANES), F32),
        compiler_params=_params("parallel", "parallel"),
        name="gdn_gates",
    )(x3d, w_gate, alog_row, dtb_row)


def _gdn_kernel(q_ref, k_ref, v_ref, z_ref, qp_ref, kp_ref, vp_ref, gates_ref,
                wq_ref, wk_ref, wv_ref, ng_ref, o_ref, state_ref,
                *, tt, nheads, hp, wave):
    hg = pl.program_id(1)
    t = pl.program_id(2)
    c = GDN_CHUNK
    sc = GDN_SUPER
    ns = tt // sc
    nc = tt // c
    dk = GDN_DK

    @pl.when(t == 0)
    def _():
        state_ref[...] = jnp.zeros_like(state_ref)

    row16 = lax.broadcasted_iota(jnp.int32, (16, LANES), 0)

    def conv_silu(x, xp, w):
        x = x.astype(F32)
        xp = jnp.where(t > 0, xp.astype(F32), 0.0)
        y = x * w[GDN_CONV - 1:GDN_CONV, :]
        for s in range(1, GDN_CONV):
            xs = pltpu.roll(x, s, 0)
            head = jnp.where(row16 < s, pltpu.roll(xp, s, 0), xs[:16])
            xs = jnp.concatenate([head, xs[16:]], axis=0)
            y = y + xs * w[GDN_CONV - 1 - s:GDN_CONV - s, :]
        return _silu(y)

    def l2n(x):
        return x * lax.rsqrt(jnp.sum(x * x, axis=-1, keepdims=True) + NORM_EPS)

    gb_all = gates_ref[0]
    lane = lax.broadcasted_iota(jnp.int32, (tt, LANES), 1)

    ri = lax.broadcasted_iota(jnp.int32, (sc, sc), 0)
    ci = lax.broadcasted_iota(jnp.int32, (sc, sc), 1)
    same = (ri // c) == (ci // c)
    tril = same & (ri >= ci)
    strict = same & (ri > ci)
    eye = (ri == ci).astype(F32)

    dm, lhs, k_s, rhs, qd_u, kd_u, cd = {}, {}, {}, {}, {}, {}, {}
    o_loc, q_eff, kuw = {}, {}, {}

    def prepare(hh):
        ln = slice(hh * dk, (hh + 1) * dk)
        hd = hg * hp + hh
        qn = l2n(conv_silu(q_ref[0, :, ln], qp_ref[0, :, ln], wq_ref[:, ln])) * (dk ** -0.5)
        kn = l2n(conv_silu(k_ref[0, :, ln], kp_ref[0, :, ln], wk_ref[:, ln]))
        vv = conv_silu(v_ref[0, :, ln], vp_ref[0, :, ln], wv_ref[:, ln])
        beta = jnp.sum(jnp.where(lane == hd, gb_all, 0.0), axis=-1, keepdims=True)
        gcol = jnp.sum(jnp.where(lane == nheads + hd, gb_all, 0.0), axis=-1, keepdims=True)
        glast = jnp.broadcast_to(gcol.reshape(nc, c, 1)[:, c - 1:c, :], (nc, c, 1)).reshape(tt, 1)
        eg = jnp.exp(gcol)
        kb = kn * beta
        kbg = kb * eg
        qd = qn * eg
        kd = kn * jnp.exp(glast - gcol)
        vb = vv * beta
        for sidx in range(ns):
            r = slice(sidx * sc, (sidx + 1) * sc)
            u = (hh, sidx)
            gb = jnp.broadcast_to(gcol[r], (sc, sc))
            dm[u] = jnp.exp(jnp.where(tril, gb - gb.T, NEG_INF))
            k_s[u] = kn[r].astype(BF16)
            lhs[u] = jnp.concatenate([kb[r], qn[r]], axis=0).astype(BF16)
            rhs[u] = jnp.concatenate([vb[r], kbg[r]], axis=1).astype(BF16)
            qd_u[u] = qd[r]
            kd_u[u] = kd[r].astype(BF16)
            for half in range(sc // c):
                row = sidx * sc + half * c
                cd[hh, sidx * (sc // c) + half] = jnp.exp(glast[row:row + 1])

    def wy_transform(units):
        kq = {u: lax.dot_general(lhs[u], k_s[u], NT_DIMS, preferred_element_type=F32) for u in units}
        attn = {u: (kq[u][sc:] * dm[u]).astype(BF16) for u in units}
        low = {u: jnp.where(strict, kq[u][:sc] * dm[u], 0.0) for u in units}
        pw = {u: low[u].astype(BF16) for u in units}
        tinv = {u: eye - low[u] for u in units}
        span = 2
        while span < c:
            pw = {u: jnp.dot(pw[u], pw[u], preferred_element_type=F32).astype(BF16) for u in units}
            tinv = {u: tinv[u] + jnp.dot(tinv[u].astype(BF16), pw[u], preferred_element_type=F32)
                    for u in units}
            span *= 2
        uw_b = {u: jnp.dot(tinv[u].astype(BF16), rhs[u], preferred_element_type=F32).astype(BF16)
                for u in units}
        auw = {u: jnp.dot(attn[u], uw_b[u], preferred_element_type=F32) for u in units}
        for u in units:
            hh, sidx = u
            qe = (qd_u[u] - auw[u][:, dk:]).astype(BF16)
            for half in range(sc // c):
                a = slice(half * c, (half + 1) * c)
                ch = sidx * (sc // c) + half
                o_loc[hh, ch] = auw[u][a, :dk]
                q_eff[hh, ch] = qe[a]
                kuw[hh, ch] = lax.dot_general(kd_u[u][a], uw_b[u][a], TN_DIMS,
                                              preferred_element_type=F32)

    for w0 in range(0, hp, wave):
        heads = range(w0, min(w0 + wave, hp))
        for hh in heads:
            prepare(hh)
        wy_transform([(hh, sidx) for hh in heads for sidx in range(ns)])

    o_parts = [[] for _ in range(hp)]
    for ch in range(nc):
        for hh in range(hp):
            st = state_ref[hh]
            lhs2 = jnp.concatenate([kuw[hh, ch][:, dk:].astype(BF16), q_eff[hh, ch]], axis=0)
            res = jnp.dot(lhs2, st.astype(BF16), preferred_element_type=F32)
            o_parts[hh].append(o_loc[hh, ch] + res[dk:])
            state_ref[hh] = st * cd[hh, ch] + kuw[hh, ch][:, :dk] - res[:dk]

    for hh in range(hp):
        ln = slice(hh * dk, (hh + 1) * dk)
        o = jnp.concatenate(o_parts[hh], axis=0)
        o = o * lax.rsqrt(jnp.mean(o * o, axis=-1, keepdims=True) + NORM_EPS) * ng_ref[...]
        o = o * _silu(z_ref[0, :, ln].astype(F32))
        o_ref[0, :, ln] = o.astype(o_ref.dtype)


def _gdn_core(qkvz, gates, w_conv, norm_g, tt, hp):
    bsz, seq, width = qkvz.shape
    nh = width // (4 * GDN_DK)
    ng = nh // hp
    hw = hp * GDN_DK
    prev = tt // 16

    def cur(off):
        return pl.BlockSpec((1, tt, hw), lambda b, h, t: (b, t, off + h))

    def prv(off):
        return pl.BlockSpec((1, 16, hw), lambda b, h, t: (b, jnp.maximum(t * prev - 1, 0), off + h))

    def cw(off):
        return pl.BlockSpec((GDN_CONV, hw), lambda b, h, t: (0, off + h))

    row = pl.BlockSpec((1, LANES), lambda b, h, t: (0, 0))
    return pl.pallas_call(
        functools.partial(_gdn_kernel, tt=tt, nheads=nh, hp=hp, wave=GDN_WAVE),
        grid=(bsz, ng, seq // tt),
        in_specs=[cur(0), cur(ng), cur(2 * ng), cur(3 * ng),
                  prv(0), prv(ng), prv(2 * ng),
                  pl.BlockSpec((1, tt, LANES), lambda b, h, t: (b, t, 0)),
                  cw(0), cw(ng), cw(2 * ng), row],
        out_specs=pl.BlockSpec((1, tt, hw), lambda b, h, t: (b, t, h)),
        out_shape=jax.ShapeDtypeStruct((bsz, seq, nh * GDN_DK), BF16),
        scratch_shapes=[pltpu.VMEM((hp, GDN_DK, GDN_DK), F32)],
        compiler_params=_params("parallel", "parallel", "arbitrary"),
        name="gdn_core",
    )(qkvz, qkvz, qkvz, qkvz, qkvz, qkvz, qkvz, gates,
      w_conv, w_conv, w_conv, norm_g.astype(F32).reshape(1, LANES))


def _gdn_layer(h2d, bsz, seq, w_in, w_conv, a_log, dt_bias, norm_g, w_out):
    nh = a_log.shape[0]
    main = 4 * nh * GDN_DK
    w_main = w_in[:, :main].astype(BF16)
    w_gate = jnp.pad(w_in[:, main:], ((0, 0), (0, LANES - 2 * nh))).astype(BF16)
    qkvz = _proj(h2d, w_main, BF16, 1024, 2048).reshape(bsz, seq, main)
    gates = _gdn_gates(h2d.reshape(bsz, seq, -1), w_gate, a_log, dt_bias, tg=512)
    o = _gdn_core(qkvz, gates, w_conv.astype(F32), norm_g, tt=512, hp=4)
    return o.reshape(bsz * seq, nh * GDN_DK), w_out.astype(BF16)


def _moba_kernel(q_ref, k_ref, v_ref, o_ref, kmh_ref, kml_ref, vt_ref, *, nb, bs, grp, hp):
    i = pl.program_id(2)
    dh = MOBA_DH
    gw = grp * bs
    heads = range(hp)

    @pl.when(i == 0)
    def _():
        for hh in heads:
            ln = slice(hh * dh, (hh + 1) * dh)
            kf = k_ref[0, :, ln].astype(F32).reshape(nb, bs, dh)
            km = jnp.sum(kf, axis=1) * (1.0 / bs)
            hi = km.astype(BF16)
            kmh_ref[hh] = hi
            kml_ref[hh] = (km - hi.astype(F32)).astype(BF16)

            def tr(j, carry):
                vj = v_ref[0, pl.ds(pl.multiple_of(j * gw, gw), gw), ln].astype(F32)
                vjt = vj.T.astype(BF16)
                for u in range(grp):
                    vt_ref[hh, j * grp + u, :dh, :] = vjt[:, u * bs:(u + 1) * bs]
                    vt_ref[hh, j * grp + u, dh:, :] = jnp.ones((ONES_ROWS, bs), BF16)
                return carry
            lax.fori_loop(0, nb // grp, tr, 0)

    q = [q_ref[0, :, hh * dh:(hh + 1) * dh] for hh in heads]
    own = pl.multiple_of(i * bs, bs)
    kpos = lax.broadcasted_iota(jnp.int32, (bs, bs), 0)
    qpos = lax.broadcasted_iota(jnp.int32, (bs, bs), 1)

    def select_bias(hh):
        gate = (lax.dot_general(kmh_ref[hh], q[hh], NT_DIMS, preferred_element_type=F32) +
                lax.dot_general(kml_ref[hh], q[hh], NT_DIMS, preferred_element_type=F32))
        blk = lax.broadcasted_iota(jnp.int32, (nb, bs), 0)
        past = blk < i
        g = jnp.where(past, gate, -jnp.inf)
        sel = jnp.zeros((nb, bs), F32)
        for _ in range(MOBA_TOPK):
            mx = jnp.max(g, axis=0, keepdims=True)
            idx = jnp.min(jnp.where(g == mx, blk, nb), axis=0, keepdims=True)
            hit = blk == idx
            sel = jnp.where(hit, 1.0, sel)
            g = jnp.where(hit, -jnp.inf, g)
        return jnp.where(past & (sel > 0.0), 0.0, NEG_INF)

    def scores(hh, row0):
        return lax.dot_general(k_ref[0, pl.ds(row0, bs), hh * dh:(hh + 1) * dh], q[hh], NT_DIMS,
                               preferred_element_type=F32)

    def attend(ngroups):
        s_own = [jnp.where(kpos <= qpos, scores(hh, own), NEG_INF) for hh in heads]
        s_next = [[scores(hh, u * bs) for u in range(grp)] if ngroups else [] for hh in heads]
        bias = [select_bias(hh) if ngroups else None for hh in heads]
        m = [jnp.max(s_own[hh], axis=0, keepdims=True) for hh in heads]
        p = [jnp.exp2(s_own[hh] - m[hh]).astype(BF16) for hh in heads]
        acc = [jnp.dot(vt_ref[hh, i], p[hh], preferred_element_type=F32) for hh in heads]
        for gi in range(ngroups):
            s_cur = s_next
            s_next = [[scores(hh, (gi + 1) * gw + u * bs) for u in range(grp)] if gi + 1 < ngroups else []
                      for hh in heads]
            brow = [[bias[hh][gi * grp + u:gi * grp + u + 1] for u in range(grp)] for hh in heads]
            m_new = list(m)
            for u in range(grp):
                for hh in heads:
                    m_new[hh] = jnp.maximum(m_new[hh],
                                            jnp.max(s_cur[hh][u], axis=0, keepdims=True) + brow[hh][u])
            acc = [jnp.exp2(m[hh] - m_new[hh]) * acc[hh] for hh in heads]
            for u in range(grp):
                for hh in heads:
                    pu = jnp.exp2((s_cur[hh][u] - (m_new[hh] - brow[hh][u])).astype(BF16))
                    acc[hh] = acc[hh] + jnp.dot(vt_ref[hh, gi * grp + u], pu, preferred_element_type=F32)
            m = m_new
        for hh in heads:
            o_ref[0, :, hh * dh:(hh + 1) * dh] = (acc[hh][:dh] / acc[hh][dh:dh + 1]).T.astype(o_ref.dtype)

    needed = (i + grp - 1) // grp
    for ngroups in range(nb // grp + 1):
        pl.when(needed == ngroups)(functools.partial(attend, ngroups))


def _moba_core(q, k, v, hp):
    bsz, seq, width = q.shape
    nh = width // MOBA_DH
    bs = MOBA_BLOCK
    nb = seq // bs
    grp = math.gcd(nb, MOBA_GROUP)
    hw = hp * MOBA_DH
    full = pl.BlockSpec((1, seq, hw), lambda b, h, i: (b, 0, h))
    blk = pl.BlockSpec((1, bs, hw), lambda b, h, i: (b, i, h))
    return pl.pallas_call(
        functools.partial(_moba_kernel, nb=nb, bs=bs, grp=grp, hp=hp),
        grid=(bsz, nh // hp, nb),
        in_specs=[blk, full, full],
        out_specs=blk,
        out_shape=jax.ShapeDtypeStruct((bsz, seq, width), BF16),
        scratch_shapes=[pltpu.VMEM((hp, nb, MOBA_DH), BF16), pltpu.VMEM((hp, nb, MOBA_DH), BF16),
                        pltpu.VMEM((hp, nb, MOBA_DH + ONES_ROWS, bs), BF16)],
        compiler_params=_params("parallel", "parallel", "arbitrary"),
        name="moba_core",
    )(q, k, v)


def _moba_rope_tables(seq):
    half = ROPE_DIMS // 2
    inv_freq = jnp.power(ROPE_THETA, -jnp.arange(half, dtype=F32) / half)
    ang = jnp.arange(seq, dtype=F32)[:, None] * inv_freq[None, :]
    cos, sin = jnp.cos(ang), jnp.sin(ang)
    rest = LANES - 2 * half
    c = jnp.concatenate([cos, cos, jnp.ones((seq, rest), F32)], axis=1)
    s1 = jnp.concatenate([-sin, jnp.zeros((seq, LANES - half), F32)], axis=1)
    s2 = jnp.concatenate([jnp.zeros((seq, half), F32), sin, jnp.zeros((seq, rest), F32)], axis=1)
    return c, s1, s2


def _moba_layer(h2d, bsz, seq, w_qkv, w_out):
    width = w_qkv.shape[1] // 3
    nh = width // MOBA_DH
    assert seq % MOBA_BLOCK == 0
    tabs = _moba_rope_tables(seq)
    wq, wk, wv = (w_qkv[:, n * width:(n + 1) * width].astype(BF16) for n in range(3))
    q = _proj_rope_partial(h2d, wq, tabs, seq, MOBA_DH ** -0.5 * math.log2(math.e), 1024, width)
    k = _proj_rope_partial(h2d, wk, tabs, seq, 1.0, 1024, width)
    v = _proj(h2d, wv, BF16, 1024, width)
    shp = (bsz, seq, width)
    o = _moba_core(q.reshape(shp), k.reshape(shp), v.reshape(shp), hp=2)
    return o.reshape(bsz * seq, nh * MOBA_DH), w_out.astype(BF16)


def _ret_kernel(lg_ref, q_ref, k_ref, v_ref, gate_ref, gn_ref, o_ref, state_ref, dmat_ref, *, c, hp):
    t = pl.program_id(2)
    dk, dv = RET_DK, RET_DV
    heads = range(hp)
    lg = [lg_ref[hh][:, :1] for hh in heads]

    @pl.when(t == 0)
    def _():
        state_ref[...] = jnp.zeros_like(state_ref)
        ri = lax.broadcasted_iota(jnp.int32, (c, c), 0)
        ci = lax.broadcasted_iota(jnp.int32, (c, c), 1)
        tril = ri >= ci
        rel = jnp.where(tril, ri - ci, 0).astype(F32)
        for hh in heads:
            dmat_ref[hh] = jnp.where(tril, jnp.exp(lg[hh] * rel), 0.0)

    pos = lax.broadcasted_iota(jnp.int32, (c, 1), 0).astype(F32)
    q = [q_ref[0, :, hh * dk:(hh + 1) * dk] for hh in heads]
    k = [k_ref[0, :, hh * dk:(hh + 1) * dk] for hh in heads]
    v = [v_ref[0, :, hh * dv:(hh + 1) * dv] for hh in heads]
    st = [state_ref[hh] for hh in heads]
    inner = [lax.dot_general(q[hh], k[hh], NT_DIMS, preferred_element_type=F32) for hh in heads]
    qd = [(q[hh].astype(F32) * jnp.exp(lg[hh] * (pos + 1.0))).astype(BF16) for hh in heads]
    kd = [(k[hh].astype(F32) * jnp.exp(lg[hh] * (c - 1.0 - pos))).astype(BF16) for hh in heads]
    cross = [jnp.dot(qd[hh], st[hh].astype(BF16), preferred_element_type=F32) for hh in heads]
    kv = [lax.dot_general(kd[hh], v[hh], TN_DIMS, preferred_element_type=F32) for hh in heads]
    intra = [jnp.dot((inner[hh] * dmat_ref[hh]).astype(BF16), v[hh], preferred_element_type=F32)
             for hh in heads]
    for hh in heads:
        state_ref[hh] = st[hh] * jnp.exp(lg[hh] * c) + kv[hh]
        o = intra[hh] + cross[hh]
        mu = jnp.mean(o, axis=-1, keepdims=True)
        d = o - mu
        var = jnp.mean(d * d, axis=-1, keepdims=True)
        o = d * lax.rsqrt(var + LN_EPS) * gn_ref[:, hh * dv:(hh + 1) * dv]
        gate = gate_ref[0, :, hh * dv:(hh + 1) * dv].astype(F32)
        o_ref[0, :, hh * dv:(hh + 1) * dv] = (_silu(gate) * o).astype(o_ref.dtype)


def _ret_core(q, k, vg, gn_g, log_gamma, hp):
    bsz, seq, _ = q.shape
    nh = log_gamma.shape[0]
    ng = nh // hp
    c = RET_TILE
    lg = jnp.broadcast_to(log_gamma.reshape(nh, 1, 1), (nh, 1, LANES)).astype(F32)
    qk_spec = pl.BlockSpec((1, c, hp * RET_DK), lambda b, h, t: (b, t, h))
    v_spec = pl.BlockSpec((1, c, hp * RET_DV), lambda b, h, t: (b, t, h))
    gate_spec = pl.BlockSpec((1, c, hp * RET_DV), lambda b, h, t: (b, t, ng + h))
    return pl.pallas_call(
        functools.partial(_ret_kernel, c=c, hp=hp),
        grid=(bsz, ng, seq // c),
        in_specs=[pl.BlockSpec((hp, 1, LANES), lambda b, h, t: (h, 0, 0)),
                  qk_spec, qk_spec, v_spec, gate_spec,
                  pl.BlockSpec((1, hp * RET_DV), lambda b, h, t: (0, h))],
        out_specs=v_spec,
        out_shape=jax.ShapeDtypeStruct((bsz, seq, nh * RET_DV), BF16),
        scratch_shapes=[pltpu.VMEM((hp, RET_DK, RET_DV), F32), pltpu.VMEM((hp, c, c), F32)],
        compiler_params=_params("parallel", "parallel", "arbitrary"),
        name="ret_core",
    )(lg, q, k, vg, vg, gn_g.astype(F32).reshape(1, nh * RET_DV))


def _ret_layer(h2d, bsz, seq, w_in, gn_g, w_out):
    nh = gn_g.shape[0] // RET_DV
    qw = nh * RET_DK
    vw = nh * RET_DV
    assert seq % RET_TILE == 0
    inv_freq = jnp.power(XPOS_BASE, -jnp.linspace(0.0, 1.0, RET_DK // 2, dtype=F32))
    ang = jnp.arange(seq, dtype=F32)[:, None] * inv_freq[None, :]
    cos, sin = jnp.cos(ang), jnp.sin(ang)
    log_gamma = jnp.log1p(-jnp.exp2(-5.0 - jnp.arange(nh, dtype=F32)))
    wb = w_in.astype(BF16)
    q = _proj_rope_full(h2d, wb[:, :qw], cos, sin, seq, RET_DK, 1.0, 1024, qw)
    k = _proj_rope_full(h2d, wb[:, qw:2 * qw], cos, sin, seq, RET_DK, RET_DK ** -0.5, 1024, qw)
    vg = _proj(h2d, wb[:, 2 * qw:], BF16, 1024, 2048)
    o = _ret_core(q.reshape(bsz, seq, qw), k.reshape(bsz, seq, qw), vg.reshape(bsz, seq, 2 * vw),
                  gn_g, log_gamma, hp=4)
    return o.reshape(bsz * seq, vw), w_out.astype(BF16)


def kernel(x, a_w_in, a_conv, a_a_log, a_dt_bias, a_norm_g, a_w_out, b_w_qkv, b_w_out,
           c_w_in, c_gn_g, c_w_out, f_w13, f_w2, ln1_g, ln1_b, ln2_g, ln2_b):
    bsz, seq, d = x.shape
    depth = f_w13.shape[0]
    alpha = (2 * depth) ** 0.25
    h = x.reshape(bsz * seq, d)
    hb = h
    for i in range(depth):
        kind, j = i % N_MIXERS, i // N_MIXERS
        if kind == 0:
            o, w_out = _gdn_layer(hb, bsz, seq, a_w_in[j], a_conv[j], a_a_log[j], a_dt_bias[j],
                                  a_norm_g[j], a_w_out[j])
        elif kind == 1:
            o, w_out = _moba_layer(hb, bsz, seq, b_w_qkv[j], b_w_out[j])
        else:
            o, w_out = _ret_layer(hb, bsz, seq, c_w_in[j], c_gn_g[j], c_w_out[j])
        h, hb = _block_tail(o, w_out, h, ln1_g[i], ln1_b[i], f_w13[i].astype(BF16), f_w2[i].astype(BF16),
                            ln2_g[i], ln2_b[i], alpha, 512, 256)
    return h.reshape(bsz, seq, d)
```

```python
import functools
import math

import jax
import jax.numpy as jnp
from jax import lax
from jax.experimental import pallas as pl
from jax.experimental.pallas import tpu as pltpu

F32 = jnp.float32
BF16 = jnp.bfloat16

N_MIXERS = 3
LANES = 128
VMEM_LIMIT = 56 * 1024 * 1024

GDN_DK = 128
GDN_CHUNK = 64
GDN_SUPER = 128
GDN_CONV = 4
GDN_WAVE = 8
MOBA_DH = 128
MOBA_BLOCK = 256
MOBA_TOPK = 3
MOBA_GROUP = 4
ONES_ROWS = 16
ROPE_THETA = 500000.0
ROPE_DIMS = MOBA_DH // 4
RET_DK = 256
RET_DV = 512
RET_TILE = 256
XPOS_BASE = 10000.0
LN_EPS = 1e-5
NORM_EPS = 1e-6
NEG_INF = -1e30

NT_DIMS = (((1,), (1,)), ((), ()))
TN_DIMS = (((0,), (0,)), ((), ()))


def _params(*semantics):
    return pltpu.CompilerParams(dimension_semantics=semantics, vmem_limit_bytes=VMEM_LIMIT)


def _silu(x):
    return x * jax.nn.sigmoid(x)


def _proj_kernel(x_ref, w_ref, o_ref):
    x = x_ref[...].astype(BF16)
    o_ref[...] = jnp.dot(x, w_ref[...], preferred_element_type=F32).astype(o_ref.dtype)


def _proj(x2d, w, out_dtype, tm, tn):
    m, k = x2d.shape
    n = w.shape[1]
    return pl.pallas_call(
        _proj_kernel,
        grid=(m // tm, n // tn),
        in_specs=[pl.BlockSpec((tm, k), lambda i, j: (i, 0)),
                  pl.BlockSpec((k, tn), lambda i, j: (0, j))],
        out_specs=pl.BlockSpec((tm, tn), lambda i, j: (i, j)),
        out_shape=jax.ShapeDtypeStruct((m, n), out_dtype),
        compiler_params=_params("parallel", "parallel"),
        name="proj",
    )(x2d, w)


def _proj_rope_partial_kernel(x_ref, w_ref, c_ref, s1_ref, s2_ref, o_ref, *, half, scale):
    x = x_ref[...].astype(BF16)
    acc = jnp.dot(x, w_ref[...], preferred_element_type=F32)
    tn = acc.shape[1]
    reps = tn // LANES
    c = jnp.tile(c_ref[...], (1, reps))
    s1 = jnp.tile(s1_ref[...], (1, reps))
    s2 = jnp.tile(s2_ref[...], (1, reps))
    out = acc * c + pltpu.roll(acc, tn - half, 1) * s1 + pltpu.roll(acc, half, 1) * s2
    o_ref[...] = (out * scale).astype(o_ref.dtype)


def _proj_rope_partial(x2d, w, tabs, seq, scale, tm, tn):
    m, k = x2d.shape
    n = w.shape[1]
    c, s1, s2 = tabs
    half = ROPE_DIMS // 2
    tpos = seq // tm
    tab_spec = pl.BlockSpec((tm, LANES), lambda i, j: (i % tpos, 0))
    return pl.pallas_call(
        functools.partial(_proj_rope_partial_kernel, half=half, scale=scale),
        grid=(m // tm, n // tn),
        in_specs=[pl.BlockSpec((tm, k), lambda i, j: (i, 0)),
                  pl.BlockSpec((k, tn), lambda i, j: (0, j)),
                  tab_spec, tab_spec, tab_spec],
        out_specs=pl.BlockSpec((tm, tn), lambda i, j: (i, j)),
        out_shape=jax.ShapeDtypeStruct((m, n), BF16),
        compiler_params=_params("parallel", "parallel"),
        name="proj_rope_partial",
    )(x2d, w, c, s1, s2)


def _proj_rope_full_kernel(x_ref, w_ref, cos_ref, sin_ref, o_ref, *, dk, scale):
    x = x_ref[...].astype(BF16)
    acc = jnp.dot(x, w_ref[...], preferred_element_type=F32)
    cos = cos_ref[...]
    sin = sin_ref[...]
    half = dk // 2
    for hh in range(acc.shape[1] // dk):
        a = acc[:, hh * dk: hh * dk + half]
        b = acc[:, hh * dk + half: (hh + 1) * dk]
        o_ref[:, hh * dk: hh * dk + half] = ((a * cos - b * sin) * scale).astype(o_ref.dtype)
        o_ref[:, hh * dk + half: (hh + 1) * dk] = ((a * sin + b * cos) * scale).astype(o_ref.dtype)


def _proj_rope_full(x2d, w, cos, sin, seq, dk, scale, tm, tn):
    m, k = x2d.shape
    n = w.shape[1]
    tpos = seq // tm
    tab_spec = pl.BlockSpec((tm, dk // 2), lambda i, j: (i % tpos, 0))
    return pl.pallas_call(
        functools.partial(_proj_rope_full_kernel, dk=dk, scale=scale),
        grid=(m // tm, n // tn),
        in_specs=[pl.BlockSpec((tm, k), lambda i, j: (i, 0)),
                  pl.BlockSpec((k, tn), lambda i, j: (0, j)),
                  tab_spec, tab_spec],
        out_specs=pl.BlockSpec((tm, tn), lambda i, j: (i, j)),
        out_shape=jax.ShapeDtypeStruct((m, n), BF16),
        compiler_params=_params("parallel", "parallel"),
        name="proj_rope_full",
    )(x2d, w, cos, sin)


def _layer_norm(y, g, b):
    mu = jnp.mean(y, axis=-1, keepdims=True)
    d = y - mu
    var = jnp.mean(d * d, axis=-1, keepdims=True)
    return d * lax.rsqrt(var + LN_EPS) * g + b


def _block_tail_kernel(o_ref, wo_ref, h_ref, g1_ref, b1_ref, w13_ref, w2_ref, g2_ref, b2_ref,
                       out_ref, outb_ref, *, alpha, dff, fc):
    mix = jnp.dot(o_ref[...], wo_ref[...], preferred_element_type=F32)
    h1 = _layer_norm(alpha * h_ref[...] + mix, g1_ref[...], b1_ref[...])
    xb = h1.astype(BF16)
    acc = alpha * h1
    for c in range(dff // fc):
        gate = jnp.dot(xb, w13_ref[:, c * fc:(c + 1) * fc], preferred_element_type=F32)
        up = jnp.dot(xb, w13_ref[:, dff + c * fc: dff + (c + 1) * fc], preferred_element_type=F32)
        act = (_silu(gate) * up).astype(BF16)
        acc = acc + jnp.dot(act, w2_ref[c * fc:(c + 1) * fc, :], preferred_element_type=F32)
    out = _layer_norm(acc, g2_ref[...], b2_ref[...])
    out_ref[...] = out
    outb_ref[...] = out.astype(BF16)


def _block_tail(o2d, wo, h2d, g1, b1, w13, w2, g2, b2, alpha, tm, fc):
    m, kin = o2d.shape
    d = wo.shape[1]
    dff = w2.shape[0]
    once = pl.Buffered(1)
    vec = pl.BlockSpec((1, d), lambda i: (0, 0))
    rows = pl.BlockSpec((tm, d), lambda i: (i, 0))
    return pl.pallas_call(
        functools.partial(_block_tail_kernel, alpha=alpha, dff=dff, fc=fc),
        grid=(m // tm,),
        in_specs=[pl.BlockSpec((tm, kin), lambda i: (i, 0)),
                  pl.BlockSpec((kin, d), lambda i: (0, 0), pipeline_mode=once),
                  rows, vec, vec,
                  pl.BlockSpec((d, 2 * dff), lambda i: (0, 0), pipeline_mode=once),
                  pl.BlockSpec((dff, d), lambda i: (0, 0), pipeline_mode=once),
                  vec, vec],
        out_specs=[rows, rows],
        out_shape=[jax.ShapeDtypeStruct((m, d), F32), jax.ShapeDtypeStruct((m, d), BF16)],
        compiler_params=_params("parallel"),
        name="block_tail",
    )(o2d, wo, h2d, g1.reshape(1, d), b1.reshape(1, d), w13, w2, g2.reshape(1, d), b2.reshape(1, d))


def _gdn_gates_kernel(x_ref, w_ref, alog_ref, dtb_ref, o_ref, *, c, nheads):
    gt = jnp.dot(x_ref[0].astype(BF16), w_ref[...], preferred_element_type=F32)
    tg = gt.shape[0]
    sp_in = gt + dtb_ref[...]
    softplus = jnp.maximum(sp_in, 0.0) + jnp.log1p(jnp.exp(-jnp.abs(sp_in)))
    g = -jnp.exp(alog_ref[...]) * softplus
    ri = lax.broadcasted_iota(jnp.int32, (tg, tg), 0)
    ci = lax.broadcasted_iota(jnp.int32, (tg, tg), 1)
    tri = jnp.where(((ri // c) == (ci // c)) & (ri >= ci), 1.0, 0.0).astype(BF16)
    g1 = g.astype(BF16)
    r1 = g - g1.astype(F32)
    g2 = r1.astype(BF16)
    g3 = (r1 - g2.astype(F32)).astype(BF16)
    gc = (jnp.dot(tri, g1, preferred_element_type=F32) + jnp.dot(tri, g2, preferred_element_type=F32)
          + jnp.dot(tri, g3, preferred_element_type=F32))
    lane = lax.broadcasted_iota(jnp.int32, (tg, LANES), 1)
    o_ref[0] = jnp.where(lane < nheads, jax.nn.sigmoid(gt), gc)


def _gdn_gates(x3d, w_gate, a_log, dt_bias, tg):
    bsz, seq, d = x3d.shape
    nh = a_log.shape[0]
    pad = LANES - 2 * nh
    alog_row = jnp.concatenate([jnp.zeros((nh,), F32), a_log.astype(F32), jnp.zeros((pad,), F32)]).reshape(1, LANES)
    dtb_row = jnp.concatenate([jnp.zeros((nh,), F32), dt_bias.astype(F32), jnp.zeros((pad,), F32)]).reshape(1, LANES)
    row = pl.BlockSpec((1, LANES), lambda b, t: (0, 0))
    return pl.pallas_call(
        functools.partial(_gdn_gates_kernel, c=GDN_CHUNK, nheads=nh),
        grid=(bsz, seq // tg),
        in_specs=[pl.BlockSpec((1, tg, d), lambda b, t: (b, t, 0)),
                  pl.BlockSpec((d, LANES), lambda b, t: (0, 0)), row, row],
        out_specs=pl.BlockSpec((1, tg, LANES), lambda b, t: (b, t, 0)),
        out_shape=jax.ShapeDtypeStruct((bsz, seq, LANES), F32),
        compiler_params=_params("parallel", "parallel"),
        name="gdn_gates",
    )(x3d, w_gate, alog_row, dtb_row)


def _gdn_kernel(q_ref, k_ref, v_ref, z_ref, qp_ref, kp_ref, vp_ref, gates_ref,
                wq_ref, wk_ref, wv_ref, ng_ref, o_ref, state_ref,
                *, tt, nheads, hp, wave):
    hg = pl.program_id(1)
    t = pl.program_id(2)
    c = GDN_CHUNK
    sc = GDN_SUPER
    ns = tt // sc
    nc = tt // c
    dk = GDN_DK

    @pl.when(t == 0)
    def _():
        state_ref[...] = jnp.zeros_like(state_ref)

    row16 = lax.broadcasted_iota(jnp.int32, (16, LANES), 0)

    def conv_silu(x, xp, w):
        x = x.astype(F32)
        xp = jnp.where(t > 0, xp.astype(F32), 0.0)
        y = x * w[GDN_CONV - 1:GDN_CONV, :]
        for s in range(1, GDN_CONV):
            xs = pltpu.roll(x, s, 0)
            head = jnp.where(row16 < s, pltpu.roll(xp, s, 0), xs[:16])
            xs = jnp.concatenate([head, xs[16:]], axis=0)
            y = y + xs * w[GDN_CONV - 1 - s:GDN_CONV - s, :]
        return _silu(y)

    def l2n(x):
        return x * lax.rsqrt(jnp.sum(x * x, axis=-1, keepdims=True) + NORM_EPS)

    gb_all = gates_ref[0]
    lane = lax.broadcasted_iota(jnp.int32, (tt, LANES), 1)

    ri = lax.broadcasted_iota(jnp.int32, (sc, sc), 0)
    ci = lax.broadcasted_iota(jnp.int32, (sc, sc), 1)
    same = (ri // c) == (ci // c)
    tril = same & (ri >= ci)
    strict = same & (ri > ci)
    eye = (ri == ci).astype(F32)

    dm, lhs, k_s, rhs, qd_u, kd_u, cd = {}, {}, {}, {}, {}, {}, {}
    o_loc, q_eff, kuw = {}, {}, {}

    def prepare(hh):
        ln = slice(hh * dk, (hh + 1) * dk)
        hd = hg * hp + hh
        qn = l2n(conv_silu(q_ref[0, :, ln], qp_ref[0, :, ln], wq_ref[:, ln])) * (dk ** -0.5)
        kn = l2n(conv_silu(k_ref[0, :, ln], kp_ref[0, :, ln], wk_ref[:, ln]))
        vv = conv_silu(v_ref[0, :, ln], vp_ref[0, :, ln], wv_ref[:, ln])
        beta = jnp.sum(jnp.where(lane == hd, gb_all, 0.0), axis=-1, keepdims=True)
        gcol = jnp.sum(jnp.where(lane == nheads + hd, gb_all, 0.0), axis=-1, keepdims=True)
        glast = jnp.broadcast_to(gcol.reshape(nc, c, 1)[:, c - 1:c, :], (nc, c, 1)).reshape(tt, 1)
        eg = jnp.exp(gcol)
        kb = kn * beta
        kbg = kb * eg
        qd = qn * eg
        kd = kn * jnp.exp(glast - gcol)
        vb = vv * beta
        for sidx in range(ns):
            r = slice(sidx * sc, (sidx + 1) * sc)
            u = (hh, sidx)
            gb = jnp.broadcast_to(gcol[r], (sc, sc))
            dm[u] = jnp.exp(jnp.where(tril, gb - gb.T, NEG_INF))
            k_s[u] = kn[r].astype(BF16)
            lhs[u] = jnp.concatenate([kb[r], qn[r]], axis=0).astype(BF16)
            rhs[u] = jnp.concatenate([vb[r], kbg[r]], axis=1).astype(BF16)
            qd_u[u] = qd[r]
            kd_u[u] = kd[r].astype(BF16)
            for half in range(sc // c):
                row = sidx * sc + half * c
                cd[hh, sidx * (sc // c) + half] = jnp.exp(glast[row:row + 1])

    def wy_transform(units):
        kq = {u: lax.dot_general(lhs[u], k_s[u], NT_DIMS, preferred_element_type=F32) for u in units}
        attn = {u: (kq[u][sc:] * dm[u]).astype(BF16) for u in units}
        low = {u: jnp.where(strict, kq[u][:sc] * dm[u], 0.0) for u in units}
        pw = {u: low[u].astype(BF16) for u in units}
        tinv = {u: eye - low[u] for u in units}
        span = 2
        while span < c:
            pw = {u: jnp.dot(pw[u], pw[u], preferred_element_type=F32).astype(BF16) for u in units}
            tinv = {u: tinv[u] + jnp.dot(tinv[u].astype(BF16), pw[u], preferred_element_type=F32)
                    for u in units}
            span *= 2
        uw_b = {u: jnp.dot(tinv[u].astype(BF16), rhs[u], preferred_element_type=F32).astype(BF16)
                for u in units}
        auw = {u: jnp.dot(attn[u], uw_b[u], preferred_element_type=F32) for u in units}
        for u in units:
            hh, sidx = u
            qe = (qd_u[u] - auw[u][:, dk:]).astype(BF16)
            for half in range(sc // c):
                a = slice(half * c, (half + 1) * c)
                ch = sidx * (sc // c) + half
                o_loc[hh, ch] = auw[u][a, :dk]
                q_eff[hh, ch] = qe[a]
                kuw[hh, ch] = lax.dot_general(kd_u[u][a], uw_b[u][a], TN_DIMS,
                                              preferred_element_type=F32)

    for w0 in range(0, hp, wave):
        heads = range(w0, min(w0 + wave, hp))
        for hh in heads:
            prepare(hh)
        wy_transform([(hh, sidx) for hh in heads for sidx in range(ns)])

    o_parts = [[] for _ in range(hp)]
    for ch in range(nc):
        for hh in range(hp):
            st = state_ref[hh]
            lhs2 = jnp.concatenate([kuw[hh, ch][:, dk:].astype(BF16), q_eff[hh, ch]], axis=0)
            res = jnp.dot(lhs2, st.astype(BF16), preferred_element_type=F32)
            o_parts[hh].append(o_loc[hh, ch] + res[dk:])
            state_ref[hh] = st * cd[hh, ch] + kuw[hh, ch][:, :dk] - res[:dk]

    for hh in range(hp):
        ln = slice(hh * dk, (hh + 1) * dk)
        o = jnp.concatenate(o_parts[hh], axis=0)
        o = o * lax.rsqrt(jnp.mean(o * o, axis=-1, keepdims=True) + NORM_EPS) * ng_ref[...]
        o = o * _silu(z_ref[0, :, ln].astype(F32))
        o_ref[0, :, ln] = o.astype(o_ref.dtype)


def _gdn_core(qkvz, gates, w_conv, norm_g, tt, hp):
    bsz, seq, width = qkvz.shape
    nh = width // (4 * GDN_DK)
    ng = nh // hp
    hw = hp * GDN_DK
    prev = tt // 16

    def cur(off):
        return pl.BlockSpec((1, tt, hw), lambda b, h, t: (b, t, off + h))

    def prv(off):
        return pl.BlockSpec((1, 16, hw), lambda b, h, t: (b, jnp.maximum(t * prev - 1, 0), off + h))

    def cw(off):
        return pl.BlockSpec((GDN_CONV, hw), lambda b, h, t: (0, off + h))

    row = pl.BlockSpec((1, LANES), lambda b, h, t: (0, 0))
    return pl.pallas_call(
        functools.partial(_gdn_kernel, tt=tt, nheads=nh, hp=hp, wave=GDN_WAVE),
        grid=(bsz, ng, seq // tt),
        in_specs=[cur(0), cur(ng), cur(2 * ng), cur(3 * ng),
                  prv(0), prv(ng), prv(2 * ng),
                  pl.BlockSpec((1, tt, LANES), lambda b, h, t: (b, t, 0)),
                  cw(0), cw(ng), cw(2 * ng), row],
        out_specs=pl.BlockSpec((1, tt, hw), lambda b, h, t: (b, t, h)),
        out_shape=jax.ShapeDtypeStruct((bsz, seq, nh * GDN_DK), BF16),
        scratch_shapes=[pltpu.VMEM((hp, GDN_DK, GDN_DK), F32)],
        compiler_params=_params("parallel", "parallel", "arbitrary"),
        name="gdn_core",
    )(qkvz, qkvz, qkvz, qkvz, qkvz, qkvz, qkvz, gates,
      w_conv, w_conv, w_conv, norm_g.astype(F32).reshape(1, LANES))


def _gdn_layer(h2d, bsz, seq, w_in, w_conv, a_log, dt_bias, norm_g, w_out):
    nh = a_log.shape[0]
    main = 4 * nh * GDN_DK
    w_main = w_in[:, :main].astype(BF16)
    w_gate = jnp.pad(w_in[:, main:], ((0, 0), (0, LANES - 2 * nh))).astype(BF16)
    qkvz = _proj(h2d, w_main, BF16, 1024, 2048).reshape(bsz, seq, main)
    gates = _gdn_gates(h2d.reshape(bsz, seq, -1), w_gate, a_log, dt_bias, tg=512)
    o = _gdn_core(qkvz, gates, w_conv.astype(F32), norm_g, tt=256, hp=8)
    return o.reshape(bsz * seq, nh * GDN_DK), w_out.astype(BF16)


def _moba_kernel(q_ref, k_ref, v_ref, o_ref, kmh_ref, kml_ref, vt_ref, *, nb, bs, grp, hp):
    i = pl.program_id(2)
    dh = MOBA_DH
    gw = grp * bs
    heads = range(hp)

    @pl.when(i == 0)
    def _():
        for hh in heads:
            ln = slice(hh * dh, (hh + 1) * dh)
            kf = k_ref[0, :, ln].astype(F32).reshape(nb, bs, dh)
            km = jnp.sum(kf, axis=1) * (1.0 / bs)
            hi = km.astype(BF16)
            kmh_ref[hh] = hi
            kml_ref[hh] = (km - hi.astype(F32)).astype(BF16)

            def tr(j, carry):
                vj = v_ref[0, pl.ds(pl.multiple_of(j * gw, gw), gw), ln].astype(F32)
                vjt = vj.T.astype(BF16)
                for u in range(grp):
                    vt_ref[hh, j * grp + u, :dh, :] = vjt[:, u * bs:(u + 1) * bs]
                    vt_ref[hh, j * grp + u, dh:, :] = jnp.ones((ONES_ROWS, bs), BF16)
                return carry
            lax.fori_loop(0, nb // grp, tr, 0)

    q = [q_ref[0, :, hh * dh:(hh + 1) * dh] for hh in heads]
    own = pl.multiple_of(i * bs, bs)
    kpos = lax.broadcasted_iota(jnp.int32, (bs, bs), 0)
    qpos = lax.broadcasted_iota(jnp.int32, (bs, bs), 1)

    def select_bias(hh):
        gate = (lax.dot_general(kmh_ref[hh], q[hh], NT_DIMS, preferred_element_type=F32) +
                lax.dot_general(kml_ref[hh], q[hh], NT_DIMS, preferred_element_type=F32))
        blk = lax.broadcasted_iota(jnp.int32, (nb, bs), 0)
        past = blk < i
        g = jnp.where(past, gate, -jnp.inf)
        sel = jnp.zeros((nb, bs), F32)
        for _ in range(MOBA_TOPK):
            mx = jnp.max(g, axis=0, keepdims=True)
            idx = jnp.min(jnp.where(g == mx, blk, nb), axis=0, keepdims=True)
            hit = blk == idx
            sel = jnp.where(hit, 1.0, sel)
            g = jnp.where(hit, -jnp.inf, g)
        return jnp.where(past & (sel > 0.0), 0.0, NEG_INF)

    def scores(hh, row0):
        return lax.dot_general(k_ref[0, pl.ds(row0, bs), hh * dh:(hh + 1) * dh], q[hh], NT_DIMS,
                               preferred_element_type=F32)

    def attend(ngroups):
        s_own = [jnp.where(kpos <= qpos, scores(hh, own), NEG_INF) for hh in heads]
        s_next = [[scores(hh, u * bs) for u in range(grp)] if ngroups else [] for hh in heads]
        bias = [select_bias(hh) if ngroups else None for hh in heads]
        m = [jnp.max(s_own[hh], axis=0, keepdims=True) for hh in heads]
        p = [jnp.exp2(s_own[hh] - m[hh]).astype(BF16) for hh in heads]
        acc = [jnp.dot(vt_ref[hh, i], p[hh], preferred_element_type=F32) for hh in heads]
        for gi in range(ngroups):
            s_cur = s_next
            s_next = [[scores(hh, (gi + 1) * gw + u * bs) for u in range(grp)] if gi + 1 < ngroups else []
                      for hh in heads]
            brow = [[bias[hh][gi * grp + u:gi * grp + u + 1] for u in range(grp)] for hh in heads]
            m_new = list(m)
            for u in range(grp):
                for hh in heads:
                    m_new[hh] = jnp.maximum(m_new[hh],
                                            jnp.max(s_cur[hh][u], axis=0, keepdims=True) + brow[hh][u])
            acc = [jnp.exp2(m[hh] - m_new[hh]) * acc[hh] for hh in heads]
            for u in range(grp):
                for hh in heads:
                    pu = jnp.exp2((s_cur[hh][u] - (m_new[hh] - brow[hh][u])).astype(BF16))
                    acc[hh] = acc[hh] + jnp.dot(vt_ref[hh, gi * grp + u], pu, preferred_element_type=F32)
            m = m_new
        for hh in heads:
            o_ref[0, :, hh * dh:(hh + 1) * dh] = (acc[hh][:dh] / acc[hh][dh:dh + 1]).T.astype(o_ref.dtype)

    needed = (i + grp - 1) // grp
    for ngroups in range(nb // grp + 1):
        pl.when(needed == ngroups)(functools.partial(attend, ngroups))


def _moba_core(q, k, v, hp):
    bsz, seq, width = q.shape
    nh = width // MOBA_DH
    bs = MOBA_BLOCK
    nb = seq // bs
    grp = math.gcd(nb, MOBA_GROUP)
    hw = hp * MOBA_DH
    full = pl.BlockSpec((1, seq, hw), lambda b, h, i: (b, 0, h))
    blk = pl.BlockSpec((1, bs, hw), lambda b, h, i: (b, i, h))
    return pl.pallas_call(
        functools.partial(_moba_kernel, nb=nb, bs=bs, grp=grp, hp=hp),
        grid=(bsz, nh // hp, nb),
        in_specs=[blk, full, full],
        out_specs=blk,
        out_shape=jax.ShapeDtypeStruct((bsz, seq, width), BF16),
        scratch_shapes=[pltpu.VMEM((hp, nb, MOBA_DH), BF16), pltpu.VMEM((hp, nb, MOBA_DH), BF16),
                        pltpu.VMEM((hp, nb, MOBA_DH + ONES_ROWS, bs), BF16)],
        compiler_params=_params("parallel", "parallel", "arbitrary"),
        name="moba_core",
    )(q, k, v)


def _moba_rope_tables(seq):
    half = ROPE_DIMS // 2
    inv_freq = jnp.power(ROPE_THETA, -jnp.arange(half, dtype=F32) / half)
    ang = jnp.arange(seq, dtype=F32)[:, None] * inv_freq[None, :]
    cos, sin = jnp.cos(ang), jnp.sin(ang)
    rest = LANES - 2 * half
    c = jnp.concatenate([cos, cos, jnp.ones((seq, rest), F32)], axis=1)
    s1 = jnp.concatenate([-sin, jnp.zeros((seq, LANES - half), F32)], axis=1)
    s2 = jnp.concatenate([jnp.zeros((seq, half), F32), sin, jnp.zeros((seq, rest), F32)], axis=1)
    return c, s1, s2


def _moba_layer(h2d, bsz, seq, w_qkv, w_out):
    width = w_qkv.shape[1] // 3
    nh = width // MOBA_DH
    assert seq % MOBA_BLOCK == 0
    tabs = _moba_rope_tables(seq)
    wq, wk, wv = (w_qkv[:, n * width:(n + 1) * width].astype(BF16) for n in range(3))
    q = _proj_rope_partial(h2d, wq, tabs, seq, MOBA_DH ** -0.5 * math.log2(math.e), 1024, width)
    k = _proj_rope_partial(h2d, wk, tabs, seq, 1.0, 1024, width)
    v = _proj(h2d, wv, BF16, 1024, width)
    shp = (bsz, seq, width)
    o = _moba_core(q.reshape(shp), k.reshape(shp), v.reshape(shp), hp=2)
    return o.reshape(bsz * seq, nh * MOBA_DH), w_out.astype(BF16)


def _ret_kernel(lg_ref, q_ref, k_ref, v_ref, gate_ref, gn_ref, o_ref, state_ref, dmat_ref, *, c, hp):
    t = pl.program_id(2)
    dk, dv = RET_DK, RET_DV
    heads = range(hp)
    lg = [lg_ref[hh][:, :1] for hh in heads]

    @pl.when(t == 0)
    def _():
        state_ref[...] = jnp.zeros_like(state_ref)
        ri = lax.broadcasted_iota(jnp.int32, (c, c), 0)
        ci = lax.broadcasted_iota(jnp.int32, (c, c), 1)
        tril = ri >= ci
        rel = jnp.where(tril, ri - ci, 0).astype(F32)
        for hh in heads:
            dmat_ref[hh] = jnp.where(tril, jnp.exp(lg[hh] * rel), 0.0)

    pos = lax.broadcasted_iota(jnp.int32, (c, 1), 0).astype(F32)
    q = [q_ref[0, :, hh * dk:(hh + 1) * dk] for hh in heads]
    k = [k_ref[0, :, hh * dk:(hh + 1) * dk] for hh in heads]
    v = [v_ref[0, :, hh * dv:(hh + 1) * dv] for hh in heads]
    st = [state_ref[hh] for hh in heads]
    inner = [lax.dot_general(q[hh], k[hh], NT_DIMS, preferred_element_type=F32) for hh in heads]
    qd = [(q[hh].astype(F32) * jnp.exp(lg[hh] * (pos + 1.0))).astype(BF16) for hh in heads]
    kd = [(k[hh].astype(F32) * jnp.exp(lg[hh] * (c - 1.0 - pos))).astype(BF16) for hh in heads]
    cross = [jnp.dot(qd[hh], st[hh].astype(BF16), preferred_element_type=F32) for hh in heads]
    kv = [lax.dot_general(kd[hh], v[hh], TN_DIMS, preferred_element_type=F32) for hh in heads]
    intra = [jnp.dot((inner[hh] * dmat_ref[hh]).astype(BF16), v[hh], preferred_element_type=F32)
             for hh in heads]
    for hh in heads:
        state_ref[hh] = st[hh] * jnp.exp(lg[hh] * c) + kv[hh]
        o = intra[hh] + cross[hh]
        mu = jnp.mean(o, axis=-1, keepdims=True)
        d = o - mu
        var = jnp.mean(d * d, axis=-1, keepdims=True)
        o = d * lax.rsqrt(var + LN_EPS) * gn_ref[:, hh * dv:(hh + 1) * dv]
        gate = gate_ref[0, :, hh * dv:(hh + 1) * dv].astype(F32)
        o_ref[0, :, hh * dv:(hh + 1) * dv] = (_silu(gate) * o).astype(o_ref.dtype)


def _ret_core(q, k, vg, gn_g, log_gamma, hp):
    bsz, seq, _ = q.shape
    nh = log_gamma.shape[0]
    ng = nh // hp
    c = RET_TILE
    lg = jnp.broadcast_to(log_gamma.reshape(nh, 1, 1), (nh, 1, LANES)).astype(F32)
    qk_spec = pl.BlockSpec((1, c, hp * RET_DK), lambda b, h, t: (b, t, h))
    v_spec = pl.BlockSpec((1, c, hp * RET_DV), lambda b, h, t: (b, t, h))
    gate_spec = pl.BlockSpec((1, c, hp * RET_DV), lambda b, h, t: (b, t, ng + h))
    return pl.pallas_call(
        functools.partial(_ret_kernel, c=c, hp=hp),
        grid=(bsz, ng, seq // c),
        in_specs=[pl.BlockSpec((hp, 1, LANES), lambda b, h, t: (h, 0, 0)),
                  qk_spec, qk_spec, v_spec, gate_spec,
                  pl.BlockSpec((1, hp * RET_DV), lambda b, h, t: (0, h))],
        out_specs=v_spec,
        out_shape=jax.ShapeDtypeStruct((bsz, seq, nh * RET_DV), BF16),
        scratch_shapes=[pltpu.VMEM((hp, RET_DK, RET_DV), F32), pltpu.VMEM((hp, c, c), F32)],
        compiler_params=_params("parallel", "parallel", "arbitrary"),
        name="ret_core",
    )(lg, q, k, vg, vg, gn_g.astype(F32).reshape(1, nh * RET_DV))


def _ret_layer(h2d, bsz, seq, w_in, gn_g, w_out):
    nh = gn_g.shape[0] // RET_DV
    qw = nh * RET_DK
    vw = nh * RET_DV
    assert seq % RET_TILE == 0
    inv_freq = jnp.power(XPOS_BASE, -jnp.linspace(0.0, 1.0, RET_DK // 2, dtype=F32))
    ang = jnp.arange(seq, dtype=F32)[:, None] * inv_freq[None, :]
    cos, sin = jnp.cos(ang), jnp.sin(ang)
    log_gamma = jnp.log1p(-jnp.exp2(-5.0 - jnp.arange(nh, dtype=F32)))
    wb = w_in.astype(BF16)
    q = _proj_rope_full(h2d, wb[:, :qw], cos, sin, seq, RET_DK, 1.0, 1024, qw)
    k = _proj_rope_full(h2d, wb[:, qw:2 * qw], cos, sin, seq, RET_DK, RET_DK ** -0.5, 1024, qw)
    vg = _proj(h2d, wb[:, 2 * qw:], BF16, 1024, 2048)
    o = _ret_core(q.reshape(bsz, seq, qw), k.reshape(bsz, seq, qw), vg.reshape(bsz, seq, 2 * vw),
                  gn_g, log_gamma, hp=4)
    return o.reshape(bsz * seq, vw), w_out.astype(BF16)


def kernel(x, a_w_in, a_conv, a_a_log, a_dt_bias, a_norm_g, a_w_out, b_w_qkv, b_w_out,
           c_w_in, c_gn_g, c_w_out, f_w13, f_w2, ln1_g, ln1_b, ln2_g, ln2_b):
    bsz, seq, d = x.shape
    depth = f_w13.shape[0]
    alpha = (2 * depth) ** 0.25
    h = x.reshape(bsz * seq, d)
    hb = h
    for i in range(depth):
        kind, j = i % N_MIXERS, i // N_MIXERS
        if kind == 0:
            o, w_out = _gdn_layer(hb, bsz, seq, a_w_in[j], a_conv[j], a_a_log[j], a_dt_bias[j],
                                  a_norm_g[j], a_w_out[j])
        elif kind == 1:
            o, w_out = _moba_layer(hb, bsz, seq, b_w_qkv[j], b_w_out[j])
        else:
            o, w_out = _ret_layer(hb, bsz, seq, c_w_in[j], c_gn_g[j], c_w_out[j])
        h, hb = _block_tail(o, w_out, h, ln1_g[i], ln1_b[i], f_w13[i].astype(BF16), f_w2[i].astype(BF16),
                            ln2_g[i], ln2_b[i], alpha, 512, 256)
    return h.reshape(bsz, seq, d)
```

```python
import functools
import math

import jax
import jax.numpy as jnp
from jax import lax
from jax.experimental import pallas as pl
from jax.experimental.pallas import tpu as pltpu

F32 = jnp.float32
BF16 = jnp.bfloat16

N_MIXERS = 3
LANES = 128
VMEM_LIMIT = 56 * 1024 * 1024

GDN_DK = 128
GDN_CHUNK = 64
GDN_SUPER = 128
GDN_CONV = 4
GDN_WAVE = 8
MOBA_DH = 128
MOBA_BLOCK = 256
MOBA_TOPK = 3
MOBA_GROUP = 4
ONES_ROWS = 16
ROPE_THETA = 500000.0
ROPE_DIMS = MOBA_DH // 4
RET_DK = 256
RET_DV = 512
RET_TILE = 256
XPOS_BASE = 10000.0
LN_EPS = 1e-5
NORM_EPS = 1e-6
NEG_INF = -1e30

NT_DIMS = (((1,), (1,)), ((), ()))
TN_DIMS = (((0,), (0,)), ((), ()))


def _params(*semantics):
    return pltpu.CompilerParams(dimension_semantics=semantics, vmem_limit_bytes=VMEM_LIMIT)


def _silu(x):
    return x * jax.nn.sigmoid(x)


def _proj_kernel(x_ref, w_ref, o_ref):
    x = x_ref[...].astype(BF16)
    o_ref[...] = jnp.dot(x, w_ref[...], preferred_element_type=F32).astype(o_ref.dtype)


def _proj(x2d, w, out_dtype, tm, tn):
    m, k = x2d.shape
    n = w.shape[1]
    return pl.pallas_call(
        _proj_kernel,
        grid=(m // tm, n // tn),
        in_specs=[pl.BlockSpec((tm, k), lambda i, j: (i, 0)),
                  pl.BlockSpec((k, tn), lambda i, j: (0, j))],
        out_specs=pl.BlockSpec((tm, tn), lambda i, j: (i, j)),
        out_shape=jax.ShapeDtypeStruct((m, n), out_dtype),
        compiler_params=_params("parallel", "parallel"),
        name="proj",
    )(x2d, w)


def _proj_rope_partial_kernel(x_ref, w_ref, c_ref, s1_ref, s2_ref, o_ref, *, half, q_scale):
    j = pl.program_id(1)
    x = x_ref[...].astype(BF16)
    acc = jnp.dot(x, w_ref[...], preferred_element_type=F32)

    @pl.when(j < 2)
    def _():
        tn = acc.shape[1]
        reps = tn // LANES
        c = jnp.tile(c_ref[...], (1, reps))
        s1 = jnp.tile(s1_ref[...], (1, reps))
        s2 = jnp.tile(s2_ref[...], (1, reps))
        out = acc * c + pltpu.roll(acc, tn - half, 1) * s1 + pltpu.roll(acc, half, 1) * s2
        o_ref[...] = (out * jnp.where(j == 0, q_scale, 1.0)).astype(o_ref.dtype)

    @pl.when(j == 2)
    def _():
        o_ref[...] = acc.astype(o_ref.dtype)


def _proj_rope_partial(x2d, w, tabs, seq, q_scale, tm, tn):
    m, k = x2d.shape
    n = w.shape[1]
    assert n == 3 * tn
    c, s1, s2 = tabs
    half = ROPE_DIMS // 2
    tpos = seq // tm
    tab_spec = pl.BlockSpec((tm, LANES), lambda i, j: (i % tpos, 0))
    return pl.pallas_call(
        functools.partial(_proj_rope_partial_kernel, half=half, q_scale=q_scale),
        grid=(m // tm, n // tn),
        in_specs=[pl.BlockSpec((tm, k), lambda i, j: (i, 0)),
                  pl.BlockSpec((k, tn), lambda i, j: (0, j)),
                  tab_spec, tab_spec, tab_spec],
        out_specs=pl.BlockSpec((tm, tn), lambda i, j: (i, j)),
        out_shape=jax.ShapeDtypeStruct((m, n), BF16),
        compiler_params=_params("parallel", "parallel"),
        name="proj_rope_partial",
    )(x2d, w, c, s1, s2)


def _proj_rope_full_kernel(x_ref, w_ref, cos_ref, sin_ref, o_ref, *, dk, scale):
    x = x_ref[...].astype(BF16)
    acc = jnp.dot(x, w_ref[...], preferred_element_type=F32)
    cos = cos_ref[...]
    sin = sin_ref[...]
    half = dk // 2
    for hh in range(acc.shape[1] // dk):
        a = acc[:, hh * dk: hh * dk + half]
        b = acc[:, hh * dk + half: (hh + 1) * dk]
        o_ref[:, hh * dk: hh * dk + half] = ((a * cos - b * sin) * scale).astype(o_ref.dtype)
        o_ref[:, hh * dk + half: (hh + 1) * dk] = ((a * sin + b * cos) * scale).astype(o_ref.dtype)


def _proj_rope_full(x2d, w, cos, sin, seq, dk, scale, tm, tn):
    m, k = x2d.shape
    n = w.shape[1]
    tpos = seq // tm
    tab_spec = pl.BlockSpec((tm, dk // 2), lambda i, j: (i % tpos, 0))
    return pl.pallas_call(
        functools.partial(_proj_rope_full_kernel, dk=dk, scale=scale),
        grid=(m // tm, n // tn),
        in_specs=[pl.BlockSpec((tm, k), lambda i, j: (i, 0)),
                  pl.BlockSpec((k, tn), lambda i, j: (0, j)),
                  tab_spec, tab_spec],
        out_specs=pl.BlockSpec((tm, tn), lambda i, j: (i, j)),
        out_shape=jax.ShapeDtypeStruct((m, n), BF16),
        compiler_params=_params("parallel", "parallel"),
        name="proj_rope_full",
    )(x2d, w, cos, sin)


def _layer_norm(y, g, b):
    mu = jnp.mean(y, axis=-1, keepdims=True)
    d = y - mu
    var = jnp.mean(d * d, axis=-1, keepdims=True)
    return d * lax.rsqrt(var + LN_EPS) * g + b


def _block_tail_kernel(o_ref, wo_ref, h_ref, g1_ref, b1_ref, w13_ref, w2_ref, g2_ref, b2_ref,
                       out_ref, outb_ref, *, alpha, dff, fc):
    mix = jnp.dot(o_ref[...], wo_ref[...], preferred_element_type=F32)
    h1 = _layer_norm(alpha * h_ref[...] + mix, g1_ref[...], b1_ref[...])
    xb = h1.astype(BF16)
    acc = alpha * h1
    for c in range(dff // fc):
        gate = jnp.dot(xb, w13_ref[:, c * fc:(c + 1) * fc], preferred_element_type=F32)
        up = jnp.dot(xb, w13_ref[:, dff + c * fc: dff + (c + 1) * fc], preferred_element_type=F32)
        act = (_silu(gate) * up).astype(BF16)
        acc = acc + jnp.dot(act, w2_ref[c * fc:(c + 1) * fc, :], preferred_element_type=F32)
    out = _layer_norm(acc, g2_ref[...], b2_ref[...])
    out_ref[...] = out
    outb_ref[...] = out.astype(BF16)


def _block_tail(o2d, wo, h2d, g1, b1, w13, w2, g2, b2, alpha, tm, fc):
    m, kin = o2d.shape
    d = wo.shape[1]
    dff = w2.shape[0]
    once = pl.Buffered(1)
    vec = pl.BlockSpec((1, d), lambda i: (0, 0))
    rows = pl.BlockSpec((tm, d), lambda i: (i, 0))
    return pl.pallas_call(
        functools.partial(_block_tail_kernel, alpha=alpha, dff=dff, fc=fc),
        grid=(m // tm,),
        in_specs=[pl.BlockSpec((tm, kin), lambda i: (i, 0)),
                  pl.BlockSpec((kin, d), lambda i: (0, 0), pipeline_mode=once),
                  rows, vec, vec,
                  pl.BlockSpec((d, 2 * dff), lambda i: (0, 0), pipeline_mode=once),
                  pl.BlockSpec((dff, d), lambda i: (0, 0), pipeline_mode=once),
                  vec, vec],
        out_specs=[rows, rows],
        out_shape=[jax.ShapeDtypeStruct((m, d), F32), jax.ShapeDtypeStruct((m, d), BF16)],
        compiler_params=_params("parallel"),
        name="block_tail",
    )(o2d, wo, h2d, g1.reshape(1, d), b1.reshape(1, d), w13, w2, g2.reshape(1, d), b2.reshape(1, d))


def _gdn_gates_kernel(x_ref, w_ref, alog_ref, dtb_ref, o_ref, *, c, nheads):
    gt = jnp.dot(x_ref[0].astype(BF16), w_ref[...], preferred_element_type=F32)
    tg = gt.shape[0]
    sp_in = gt + dtb_ref[...]
    softplus = jnp.maximum(sp_in, 0.0) + jnp.log1p(jnp.exp(-jnp.abs(sp_in)))
    g = -jnp.exp(alog_ref[...]) * softplus
    ri = lax.broadcasted_iota(jnp.int32, (tg, tg), 0)
    ci = lax.broadcasted_iota(jnp.int32, (tg, tg), 1)
    tri = jnp.where(((ri // c) == (ci // c)) & (ri >= ci), 1.0, 0.0).astype(BF16)
    g1 = g.astype(BF16)
    r1 = g - g1.astype(F32)
    g2 = r1.astype(BF16)
    g3 = (r1 - g2.astype(F32)).astype(BF16)
    gc = (jnp.dot(tri, g1, preferred_element_type=F32) + jnp.dot(tri, g2, preferred_element_type=F32)
          + jnp.dot(tri, g3, preferred_element_type=F32))
    lane = lax.broadcasted_iota(jnp.int32, (tg, LANES), 1)
    o_ref[0] = jnp.where(lane < nheads, jax.nn.sigmoid(gt), gc)


def _gdn_gates(x3d, w_gate, a_log, dt_bias, tg):
    bsz, seq, d = x3d.shape
    nh = a_log.shape[0]
    pad = LANES - 2 * nh
    alog_row = jnp.concatenate([jnp.zeros((nh,), F32), a_log.astype(F32), jnp.zeros((pad,), F32)]).reshape(1, LANES)
    dtb_row = jnp.concatenate([jnp.zeros((nh,), F32), dt_bias.astype(F32), jnp.zeros((pad,), F32)]).reshape(1, LANES)
    row = pl.BlockSpec((1, LANES), lambda b, t: (0, 0))
    return pl.pallas_call(
        functools.partial(_gdn_gates_kernel, c=GDN_CHUNK, nheads=nh),
        grid=(bsz, seq // tg),
        in_specs=[pl.BlockSpec((1, tg, d), lambda b, t: (b, t, 0)),
                  pl.BlockSpec((d, LANES), lambda b, t: (0, 0)), row, row],
        out_specs=pl.BlockSpec((1, tg, LANES), lambda b, t: (b, t, 0)),
        out_shape=jax.ShapeDtypeStruct((bsz, seq, LANES), F32),
        compiler_params=_params("parallel", "parallel"),
        name="gdn_gates",
    )(x3d, w_gate, alog_row, dtb_row)


def _gdn_kernel(q_ref, k_ref, v_ref, z_ref, qp_ref, kp_ref, vp_ref, gates_ref,
                wq_ref, wk_ref, wv_ref, ng_ref, o_ref, state_ref,
                *, tt, nheads, hp, wave):
    hg = pl.program_id(1)
    t = pl.program_id(2)
    c = GDN_CHUNK
    sc = GDN_SUPER
    ns = tt // sc
    nc = tt // c
    dk = GDN_DK

    @pl.when(t == 0)
    def _():
        state_ref[...] = jnp.zeros_like(state_ref)

    row16 = lax.broadcasted_iota(jnp.int32, (16, LANES), 0)

    def conv_silu(x, xp, w):
        x = x.astype(F32)
        xp = jnp.where(t > 0, xp.astype(F32), 0.0)
        y = x * w[GDN_CONV - 1:GDN_CONV, :]
        for s in range(1, GDN_CONV):
            xs = pltpu.roll(x, s, 0)
            head = jnp.where(row16 < s, pltpu.roll(xp, s, 0), xs[:16])
            xs = jnp.concatenate([head, xs[16:]], axis=0)
            y = y + xs * w[GDN_CONV - 1 - s:GDN_CONV - s, :]
        return _silu(y)

    def l2n(x):
        return x * lax.rsqrt(jnp.sum(x * x, axis=-1, keepdims=True) + NORM_EPS)

    gb_all = gates_ref[0]
    lane = lax.broadcasted_iota(jnp.int32, (tt, LANES), 1)

    ri = lax.broadcasted_iota(jnp.int32, (sc, sc), 0)
    ci = lax.broadcasted_iota(jnp.int32, (sc, sc), 1)
    same = (ri // c) == (ci // c)
    tril = same & (ri >= ci)
    strict = same & (ri > ci)
    eye = (ri == ci).astype(F32)

    dm, lhs, k_s, rhs, qd_u, kd_u, cd = {}, {}, {}, {}, {}, {}, {}
    o_loc, q_eff, kuw = {}, {}, {}

    def prepare(hh):
        ln = slice(hh * dk, (hh + 1) * dk)
        hd = hg * hp + hh
        qn = l2n(conv_silu(q_ref[0, :, ln], qp_ref[0, :, ln], wq_ref[:, ln])) * (dk ** -0.5)
        kn = l2n(conv_silu(k_ref[0, :, ln], kp_ref[0, :, ln], wk_ref[:, ln]))
        vv = conv_silu(v_ref[0, :, ln], vp_ref[0, :, ln], wv_ref[:, ln])
        beta = jnp.sum(jnp.where(lane == hd, gb_all, 0.0), axis=-1, keepdims=True)
        gcol = jnp.sum(jnp.where(lane == nheads + hd, gb_all, 0.0), axis=-1, keepdims=True)
        glast = jnp.broadcast_to(gcol.reshape(nc, c, 1)[:, c - 1:c, :], (nc, c, 1)).reshape(tt, 1)
        eg = jnp.exp(gcol)
        kb = kn * beta
        kbg = kb * eg
        qd = qn * eg
        kd = kn * jnp.exp(glast - gcol)
        vb = vv * beta
        for sidx in range(ns):
            r = slice(sidx * sc, (sidx + 1) * sc)
            u = (hh, sidx)
            gb = jnp.broadcast_to(gcol[r], (sc, sc))
            dm[u] = jnp.exp(jnp.where(tril, gb - gb.T, NEG_INF))
            k_s[u] = kn[r].astype(BF16)
            lhs[u] = jnp.concatenate([kb[r], qn[r]], axis=0).astype(BF16)
            rhs[u] = jnp.concatenate([vb[r], kbg[r]], axis=1).astype(BF16)
            qd_u[u] = qd[r]
            kd_u[u] = kd[r].astype(BF16)
            for half in range(sc // c):
                row = sidx * sc + half * c
                cd[hh, sidx * (sc // c) + half] = jnp.exp(glast[row:row + 1])

    def wy_transform(units):
        kq = {u: lax.dot_general(lhs[u], k_s[u], NT_DIMS, preferred_element_type=F32) for u in units}
        attn = {u: (kq[u][sc:] * dm[u]).astype(BF16) for u in units}
        low = {u: jnp.where(strict, kq[u][:sc] * dm[u], 0.0) for u in units}
        pw = {u: low[u].astype(BF16) for u in units}
        tinv = {u: eye - low[u] for u in units}
        span = 2
        while span < c:
            pw = {u: jnp.dot(pw[u], pw[u], preferred_element_type=F32).astype(BF16) for u in units}
            tinv = {u: tinv[u] + jnp.dot(tinv[u].astype(BF16), pw[u], preferred_element_type=F32)
                    for u in units}
            span *= 2
        uw_b = {u: jnp.dot(tinv[u].astype(BF16), rhs[u], preferred_element_type=F32).astype(BF16)
                for u in units}
        auw = {u: jnp.dot(attn[u], uw_b[u], preferred_element_type=F32) for u in units}
        for u in units:
            hh, sidx = u
            qe = (qd_u[u] - auw[u][:, dk:]).astype(BF16)
            for half in range(sc // c):
                a = slice(half * c, (half + 1) * c)
                ch = sidx * (sc // c) + half
                o_loc[hh, ch] = auw[u][a, :dk]
                q_eff[hh, ch] = qe[a]
                kuw[hh, ch] = lax.dot_general(kd_u[u][a], uw_b[u][a], TN_DIMS,
                                              preferred_element_type=F32)

    for w0 in range(0, hp, wave):
        heads = range(w0, min(w0 + wave, hp))
        for hh in heads:
            prepare(hh)
        wy_transform([(hh, sidx) for hh in heads for sidx in range(ns)])

    o_parts = [[] for _ in range(hp)]
    for ch in range(nc):
        for hh in range(hp):
            st = state_ref[hh]
            lhs2 = jnp.concatenate([kuw[hh, ch][:, dk:].astype(BF16), q_eff[hh, ch]], axis=0)
            res = jnp.dot(lhs2, st.astype(BF16), preferred_element_type=F32)
            o_parts[hh].append(o_loc[hh, ch] + res[dk:])
            state_ref[hh] = st * cd[hh, ch] + kuw[hh, ch][:, :dk] - res[:dk]

    for hh in range(hp):
        ln = slice(hh * dk, (hh + 1) * dk)
        o = jnp.concatenate(o_parts[hh], axis=0)
        o = o * lax.rsqrt(jnp.mean(o * o, axis=-1, keepdims=True) + NORM_EPS) * ng_ref[...]
        o = o * _silu(z_ref[0, :, ln].astype(F32))
        o_ref[0, :, ln] = o.astype(o_ref.dtype)


def _gdn_core(qkvz, gates, w_conv, norm_g, tt, hp):
    bsz, seq, width = qkvz.shape
    nh = width // (4 * GDN_DK)
    ng = nh // hp
    hw = hp * GDN_DK
    prev = tt // 16

    def cur(off):
        return pl.BlockSpec((1, tt, hw), lambda b, h, t: (b, t, off + h))

    def prv(off):
        return pl.BlockSpec((1, 16, hw), lambda b, h, t: (b, jnp.maximum(t * prev - 1, 0), off + h))

    def cw(off):
        return pl.BlockSpec((GDN_CONV, hw), lambda b, h, t: (0, off + h))

    row = pl.BlockSpec((1, LANES), lambda b, h, t: (0, 0))
    return pl.pallas_call(
        functools.partial(_gdn_kernel, tt=tt, nheads=nh, hp=hp, wave=GDN_WAVE),
        grid=(bsz, ng, seq // tt),
        in_specs=[cur(0), cur(ng), cur(2 * ng), cur(3 * ng),
                  prv(0), prv(ng), prv(2 * ng),
                  pl.BlockSpec((1, tt, LANES), lambda b, h, t: (b, t, 0)),
                  cw(0), cw(ng), cw(2 * ng), row],
        out_specs=pl.BlockSpec((1, tt, hw), lambda b, h, t: (b, t, h)),
        out_shape=jax.ShapeDtypeStruct((bsz, seq, nh * GDN_DK), BF16),
        scratch_shapes=[pltpu.VMEM((hp, GDN_DK, GDN_DK), F32)],
        compiler_params=_params("parallel", "parallel", "arbitrary"),
        name="gdn_core",
    )(qkvz, qkvz, qkvz, qkvz, qkvz, qkvz, qkvz, gates,
      w_conv, w_conv, w_conv, norm_g.astype(F32).reshape(1, LANES))


def _gdn_layer(h2d, bsz, seq, w_in, w_conv, a_log, dt_bias, norm_g, w_out):
    nh = a_log.shape[0]
    main = 4 * nh * GDN_DK
    w_main = w_in[:, :main].astype(BF16)
    w_gate = jnp.pad(w_in[:, main:], ((0, 0), (0, LANES - 2 * nh))).astype(BF16)
    qkvz = _proj(h2d, w_main, BF16, 1024, 2048).reshape(bsz, seq, main)
    gates = _gdn_gates(h2d.reshape(bsz, seq, -1), w_gate, a_log, dt_bias, tg=512)
    o = _gdn_core(qkvz, gates, w_conv.astype(F32), norm_g, tt=256, hp=8)
    return o.reshape(bsz * seq, nh * GDN_DK), w_out.astype(BF16)


def _moba_kernel(q_ref, k_ref, v_ref, o_ref, kmh_ref, kml_ref, vt_ref, *, nb, bs, grp, hp):
    i = pl.program_id(2)
    dh = MOBA_DH
    gw = grp * bs
    heads = range(hp)

    @pl.when(i == 0)
    def _():
        for hh in heads:
            ln = slice(hh * dh, (hh + 1) * dh)
            kf = k_ref[0, :, ln].astype(F32).reshape(nb, bs, dh)
            km = jnp.sum(kf, axis=1) * (1.0 / bs)
            hi = km.astype(BF16)
            kmh_ref[hh] = hi
            kml_ref[hh] = (km - hi.astype(F32)).astype(BF16)

            def tr(j, carry):
                vj = v_ref[0, pl.ds(pl.multiple_of(j * gw, gw), gw), ln].astype(F32)
                vjt = vj.T.astype(BF16)
                for u in range(grp):
                    vt_ref[hh, j * grp + u, :dh, :] = vjt[:, u * bs:(u + 1) * bs]
                    vt_ref[hh, j * grp + u, dh:, :] = jnp.ones((ONES_ROWS, bs), BF16)
                return carry
            lax.fori_loop(0, nb // grp, tr, 0)

    q = [q_ref[0, :, hh * dh:(hh + 1) * dh] for hh in heads]
    own = pl.multiple_of(i * bs, bs)
    kpos = lax.broadcasted_iota(jnp.int32, (bs, bs), 0)
    qpos = lax.broadcasted_iota(jnp.int32, (bs, bs), 1)

    def select_bias(hh):
        gate = (lax.dot_general(kmh_ref[hh], q[hh], NT_DIMS, preferred_element_type=F32) +
                lax.dot_general(kml_ref[hh], q[hh], NT_DIMS, preferred_element_type=F32))
        blk = lax.broadcasted_iota(jnp.int32, (nb, bs), 0)
        past = blk < i
        g = jnp.where(past, gate, -jnp.inf)
        sel = jnp.zeros((nb, bs), F32)
        for _ in range(MOBA_TOPK):
            mx = jnp.max(g, axis=0, keepdims=True)
            idx = jnp.min(jnp.where(g == mx, blk, nb), axis=0, keepdims=True)
            hit = blk == idx
            sel = jnp.where(hit, 1.0, sel)
            g = jnp.where(hit, -jnp.inf, g)
        return jnp.where(past & (sel > 0.0), 0.0, NEG_INF)

    def scores(hh, row0):
        return lax.dot_general(k_ref[0, pl.ds(row0, bs), hh * dh:(hh + 1) * dh], q[hh], NT_DIMS,
                               preferred_element_type=F32)

    def attend(ngroups):
        s_own = [jnp.where(kpos <= qpos, scores(hh, own), NEG_INF) for hh in heads]
        s_next = [[scores(hh, u * bs) for u in range(grp)] if ngroups else [] for hh in heads]
        bias = [select_bias(hh) if ngroups else None for hh in heads]
        m = [jnp.max(s_own[hh], axis=0, keepdims=True) for hh in heads]
        p = [jnp.exp2(s_own[hh] - m[hh]).astype(BF16) for hh in heads]
        acc = [jnp.dot(vt_ref[hh, i], p[hh], preferred_element_type=F32) for hh in heads]
        for gi in range(ngroups):
            s_cur = s_next
            s_next = [[scores(hh, (gi + 1) * gw + u * bs) for u in range(grp)] if gi + 1 < ngroups else []
                      for hh in heads]
            brow = [[bias[hh][gi * grp + u:gi * grp + u + 1] for u in range(grp)] for hh in heads]
            m_new = list(m)
            for u in range(grp):
                for hh in heads:
                    m_new[hh] = jnp.maximum(m_new[hh],
                                            jnp.max(s_cur[hh][u], axis=0, keepdims=True) + brow[hh][u])
            acc = [jnp.exp2(m[hh] - m_new[hh]) * acc[hh] for hh in heads]
            for u in range(grp):
                for hh in heads:
                    pu = jnp.exp2((s_cur[hh][u] - (m_new[hh] - brow[hh][u])).astype(BF16))
                    acc[hh] = acc[hh] + jnp.dot(vt_ref[hh, gi * grp + u], pu, preferred_element_type=F32)
            m = m_new
        for hh in heads:
            o_ref[0, :, hh * dh:(hh + 1) * dh] = (acc[hh][:dh] / acc[hh][dh:dh + 1]).T.astype(o_ref.dtype)

    needed = (i + grp - 1) // grp
    for ngroups in range(nb // grp + 1):
        pl.when(needed == ngroups)(functools.partial(attend, ngroups))


def _moba_core(qkv, hp):
    bsz, seq, width3 = qkv.shape
    width = width3 // 3
    nh = width // MOBA_DH
    ng = nh // hp
    bs = MOBA_BLOCK
    nb = seq // bs
    grp = math.gcd(nb, MOBA_GROUP)
    hw = hp * MOBA_DH
    blk = pl.BlockSpec((1, bs, hw), lambda b, h, i: (b, i, h))
    return pl.pallas_call(
        functools.partial(_moba_kernel, nb=nb, bs=bs, grp=grp, hp=hp),
        grid=(bsz, ng, nb),
        in_specs=[blk,
                  pl.BlockSpec((1, seq, hw), lambda b, h, i: (b, 0, ng + h)),
                  pl.BlockSpec((1, seq, hw), lambda b, h, i: (b, 0, 2 * ng + h))],
        out_specs=blk,
        out_shape=jax.ShapeDtypeStruct((bsz, seq, width), BF16),
        scratch_shapes=[pltpu.VMEM((hp, nb, MOBA_DH), BF16), pltpu.VMEM((hp, nb, MOBA_DH), BF16),
                        pltpu.VMEM((hp, nb, MOBA_DH + ONES_ROWS, bs), BF16)],
        compiler_params=_params("parallel", "parallel", "arbitrary"),
        name="moba_core",
    )(qkv, qkv, qkv)


def _moba_rope_tables(seq):
    half = ROPE_DIMS // 2
    inv_freq = jnp.power(ROPE_THETA, -jnp.arange(half, dtype=F32) / half)
    ang = jnp.arange(seq, dtype=F32)[:, None] * inv_freq[None, :]
    cos, sin = jnp.cos(ang), jnp.sin(ang)
    rest = LANES - 2 * half
    c = jnp.concatenate([cos, cos, jnp.ones((seq, rest), F32)], axis=1)
    s1 = jnp.concatenate([-sin, jnp.zeros((seq, LANES - half), F32)], axis=1)
    s2 = jnp.concatenate([jnp.zeros((seq, half), F32), sin, jnp.zeros((seq, rest), F32)], axis=1)
    return c, s1, s2


def _moba_layer(h2d, bsz, seq, w_qkv, w_out):
    width = w_qkv.shape[1] // 3
    nh = width // MOBA_DH
    assert seq % MOBA_BLOCK == 0
    tabs = _moba_rope_tables(seq)
    qkv = _proj_rope_partial(h2d, w_qkv.astype(BF16), tabs, seq, MOBA_DH ** -0.5 * math.log2(math.e),
                             1024, width)
    o = _moba_core(qkv.reshape(bsz, seq, 3 * width), hp=2)
    return o.reshape(bsz * seq, nh * MOBA_DH), w_out.astype(BF16)


def _ret_kernel(lg_ref, q_ref, k_ref, v_ref, gate_ref, gn_ref, o_ref, state_ref, dmat_ref, *, c, hp):
    t = pl.program_id(2)
    dk, dv = RET_DK, RET_DV
    heads = range(hp)
    lg = [lg_ref[hh][:, :1] for hh in heads]

    @pl.when(t == 0)
    def _():
        state_ref[...] = jnp.zeros_like(state_ref)
        ri = lax.broadcasted_iota(jnp.int32, (c, c), 0)
        ci = lax.broadcasted_iota(jnp.int32, (c, c), 1)
        tril = ri >= ci
        rel = jnp.where(tril, ri - ci, 0).astype(F32)
        for hh in heads:
            dmat_ref[hh] = jnp.where(tril, jnp.exp(lg[hh] * rel), 0.0)

    pos = lax.broadcasted_iota(jnp.int32, (c, 1), 0).astype(F32)
    q = [q_ref[0, :, hh * dk:(hh + 1) * dk] for hh in heads]
    k = [k_ref[0, :, hh * dk:(hh + 1) * dk] for hh in heads]
    v = [v_ref[0, :, hh * dv:(hh + 1) * dv] for hh in heads]
    st = [state_ref[hh] for hh in heads]
    inner = [lax.dot_general(q[hh], k[hh], NT_DIMS, preferred_element_type=F32) for hh in heads]
    qd = [(q[hh].astype(F32) * jnp.exp(lg[hh] * (pos + 1.0))).astype(BF16) for hh in heads]
    kd = [(k[hh].astype(F32) * jnp.exp(lg[hh] * (c - 1.0 - pos))).astype(BF16) for hh in heads]
    cross = [jnp.dot(qd[hh], st[hh].astype(BF16), preferred_element_type=F32) for hh in heads]
    kv = [lax.dot_general(kd[hh], v[hh], TN_DIMS, preferred_element_type=F32) for hh in heads]
    intra = [jnp.dot((inner[hh] * dmat_ref[hh]).astype(BF16), v[hh], preferred_element_type=F32)
             for hh in heads]
    for hh in heads:
        state_ref[hh] = st[hh] * jnp.exp(lg[hh] * c) + kv[hh]
        o = intra[hh] + cross[hh]
        mu = jnp.mean(o, axis=-1, keepdims=True)
        d = o - mu
        var = jnp.mean(d * d, axis=-1, keepdims=True)
        o = d * lax.rsqrt(var + LN_EPS) * gn_ref[:, hh * dv:(hh + 1) * dv]
        gate = gate_ref[0, :, hh * dv:(hh + 1) * dv].astype(F32)
        o_ref[0, :, hh * dv:(hh + 1) * dv] = (_silu(gate) * o).astype(o_ref.dtype)


def _ret_core(q, k, vg, gn_g, log_gamma, hp):
    bsz, seq, _ = q.shape
    nh = log_gamma.shape[0]
    ng = nh // hp
    c = RET_TILE
    lg = jnp.broadcast_to(log_gamma.reshape(nh, 1, 1), (nh, 1, LANES)).astype(F32)
    qk_spec = pl.BlockSpec((1, c, hp * RET_DK), lambda b, h, t: (b, t, h))
    v_spec = pl.BlockSpec((1, c, hp * RET_DV), lambda b, h, t: (b, t, h))
    gate_spec = pl.BlockSpec((1, c, hp * RET_DV), lambda b, h, t: (b, t, ng + h))
    return pl.pallas_call(
        functools.partial(_ret_kernel, c=c, hp=hp),
        grid=(bsz, ng, seq // c),
        in_specs=[pl.BlockSpec((hp, 1, LANES), lambda b, h, t: (h, 0, 0)),
                  qk_spec, qk_spec, v_spec, gate_spec,
                  pl.BlockSpec((1, hp * RET_DV), lambda b, h, t: (0, h))],
        out_specs=v_spec,
        out_shape=jax.ShapeDtypeStruct((bsz, seq, nh * RET_DV), BF16),
        scratch_shapes=[pltpu.VMEM((hp, RET_DK, RET_DV), F32), pltpu.VMEM((hp, c, c), F32)],
        compiler_params=_params("parallel", "parallel", "arbitrary"),
        name="ret_core",
    )(lg, q, k, vg, vg, gn_g.astype(F32).reshape(1, nh * RET_DV))


def _ret_layer(h2d, bsz, seq, w_in, gn_g, w_out):
    nh = gn_g.shape[0] // RET_DV
    qw = nh * RET_DK
    vw = nh * RET_DV
    assert seq % RET_TILE == 0
    inv_freq = jnp.power(XPOS_BASE, -jnp.linspace(0.0, 1.0, RET_DK // 2, dtype=F32))
    ang = jnp.arange(seq, dtype=F32)[:, None] * inv_freq[None, :]
    cos, sin = jnp.cos(ang), jnp.sin(ang)
    log_gamma = jnp.log1p(-jnp.exp2(-5.0 - jnp.arange(nh, dtype=F32)))
    wb = w_in.astype(BF16)
    q = _proj_rope_full(h2d, wb[:, :qw], cos, sin, seq, RET_DK, 1.0, 1024, qw)
    k = _proj_rope_full(h2d, wb[:, qw:2 * qw], cos, sin, seq, RET_DK, RET_DK ** -0.5, 1024, qw)
    vg = _proj(h2d, wb[:, 2 * qw:], BF16, 1024, 2048)
    o = _ret_core(q.reshape(bsz, seq, qw), k.reshape(bsz, seq, qw), vg.reshape(bsz, seq, 2 * vw),
                  gn_g, log_gamma, hp=4)
    return o.reshape(bsz * seq, vw), w_out.astype(BF16)


def kernel(x, a_w_in, a_conv, a_a_log, a_dt_bias, a_norm_g, a_w_out, b_w_qkv, b_w_out,
           c_w_in, c_gn_g, c_w_out, f_w13, f_w2, ln1_g, ln1_b, ln2_g, ln2_b):
    bsz, seq, d = x.shape
    depth = f_w13.shape[0]
    alpha = (2 * depth) ** 0.25
    h = x.reshape(bsz * seq, d)
    hb = h
    for i in range(depth):
        kind, j = i % N_MIXERS, i // N_MIXERS
        if kind == 0:
            o, w_out = _gdn_layer(hb, bsz, seq, a_w_in[j], a_conv[j], a_a_log[j], a_dt_bias[j],
                                  a_norm_g[j], a_w_out[j])
        elif kind == 1:
            o, w_out = _moba_layer(hb, bsz, seq, b_w_qkv[j], b_w_out[j])
        else:
            o, w_out = _ret_layer(hb, bsz, seq, c_w_in[j], c_gn_g[j], c_w_out[j])
        h, hb = _block_tail(o, w_out, h, ln1_g[i], ln1_b[i], f_w13[i].astype(BF16), f_w2[i].astype(BF16),
                            ln2_g[i], ln2_b[i], alpha, 512, 256)
    return h.reshape(bsz, seq, d)
```
